```python
import math
import jax, jax.numpy as jnp
from jax import lax
import numpy as np

D_MODEL = 1024
BATCH = 4
SEQ = 8192
DEPTH = 2

N_ATTN = (DEPTH + 1) // 2
N_LRU = DEPTH // 2
N_DENSE = (DEPTH + 1) // 2
N_MOE = DEPTH // 2

N_HEADS = 8
HEAD_DIM = D_MODEL // N_HEADS // 2
V_DIM = 2 * HEAD_DIM
Q_BLOCK = 128
SCALE = HEAD_DIM ** -0.5
NEG_INF = -1e30

NUM_BUCKETS = 32
MAX_EXACT = NUM_BUCKETS // 2
MAX_DISTANCE = 128

D_RNN = 1280
N_LRU_BLOCKS = 10
LRU_BLOCK = D_RNN // N_LRU_BLOCKS
CONV_WIDTH = 4
LRU_C = 8.0

D_FF = 2816
N_EXPERTS = 8
TOP_K = 2
D_FF_EXPERT = 3584

NORM_EPS = 1e-6
SUBLN_EPS = 1e-5

kernel_name = "hybrid_diffattn_rglru_moe"


def rmsnorm(x, g, eps=NORM_EPS):
    xf = x.astype(jnp.float32)
    y = xf * lax.rsqrt(jnp.mean(xf * xf, axis=-1, keepdims=True) + eps)
    return (y * g.astype(jnp.float32)).astype(x.dtype)


def t5_bucket(n):
    nf = jnp.maximum(n, 1).astype(jnp.float32)
    large = MAX_EXACT + (jnp.log(nf / MAX_EXACT) / math.log(MAX_DISTANCE / MAX_EXACT)
                         * (NUM_BUCKETS - MAX_EXACT)).astype(jnp.int32)
    large = jnp.minimum(large, NUM_BUCKETS - 1)
    return jnp.where(n < MAX_EXACT, n, large)


def diff_attention(h, w_qkv, w_o, lq1, lk1, lq2, lk2, subln_g, rel_bias, lam_init):
    B, S, _ = h.shape
    q, k, v = jnp.split(h @ w_qkv, 3, axis=-1)
    q = q.reshape(B, S, N_HEADS, 2, HEAD_DIM)
    k = k.reshape(B, S, N_HEADS, 2, HEAD_DIM)
    v = v.reshape(B, S, N_HEADS, V_DIM)
    lam = (jnp.exp(jnp.sum(lq1.astype(jnp.float32) * lk1.astype(jnp.float32)))
           - jnp.exp(jnp.sum(lq2.astype(jnp.float32) * lk2.astype(jnp.float32)))
           + lam_init)
    nb = S // Q_BLOCK
    qb = q.reshape(B, nb, Q_BLOCK, N_HEADS, 2, HEAD_DIM).transpose(1, 0, 2, 3, 4, 5)
    k_pos = jnp.arange(S, dtype=jnp.int32)

    def block(args):
        q_blk, blk = args
        q_pos = blk * Q_BLOCK + jnp.arange(Q_BLOCK, dtype=jnp.int32)
        dist = q_pos[:, None] - k_pos[None, :]
        bias = rel_bias[t5_bucket(jnp.maximum(dist, 0))]
        bias = jnp.transpose(bias, (2, 0, 1)).astype(jnp.float32)[None, :, None]
        s = jnp.einsum('bqhcd,bkhcd->bhcqk', q_blk, k).astype(jnp.float32) * SCALE + bias
        s = jnp.where(dist >= 0, s, NEG_INF)
        p = jax.nn.softmax(s, axis=-1)
        a = p[:, :, 0] - lam * p[:, :, 1]
        return jnp.einsum('bhqk,bkhe->bqhe', a.astype(v.dtype), v)

    o = lax.map(block, (qb, jnp.arange(nb, dtype=jnp.int32)))
    o = o.transpose(1, 0, 2, 3, 4).reshape(B, S, N_HEADS, V_DIM)
    o = rmsnorm(o, subln_g, SUBLN_EPS) * (1.0 - lam_init)
    return o.reshape(B, S, N_HEADS * V_DIM) @ w_o


def _lin_combine(c1, c2):
    a1, b1 = c1
    a2, b2 = c2
    return a1 * a2, a2 * b1 + b2


def rglru_block(h, w_in, conv_w, conv_b, ga_w, ga_b, gx_w, gx_b, lam_param, w_out):
    B, S, _ = h.shape
    gate_br, x_br = jnp.split(h @ w_in, 2, axis=-1)
    gate_br = jax.nn.gelu(gate_br)
    xp = jnp.pad(x_br, ((0, 0), (CONV_WIDTH - 1, 0), (0, 0)))
    xc = conv_b + sum(xp[:, j:j + S] * conv_w[j] for j in range(CONV_WIDTH))
    xblk = xc.reshape(B, S, N_LRU_BLOCKS, LRU_BLOCK)
    r = jax.nn.sigmoid(jnp.einsum('bsnd,nde->bsne', xblk, ga_w).reshape(B, S, D_RNN) + ga_b)
    i = jax.nn.sigmoid(jnp.einsum('bsnd,nde->bsne', xblk, gx_w).reshape(B, S, D_RNN) + gx_b)
    log_a = -LRU_C * r.astype(jnp.float32) * jax.nn.softplus(-lam_param.astype(jnp.float32))
    a = jnp.exp(log_a)
    mult = jnp.sqrt(-jnp.expm1(2.0 * log_a))
    b = mult * (i * xc).astype(jnp.float32)
    _, hs = lax.associative_scan(_lin_combine, (a, b), axis=1)
    y = hs.astype(h.dtype) * gate_br
    return y @ w_out


def swiglu(t, wg, wu, wd):
    return (jax.nn.silu(t @ wg) * (t @ wu)) @ wd


def moe_swiglu(h, router, wg, wu, wd):
    B, S, D = h.shape
    t = h.reshape(B * S, D)
    logits = (t @ router).astype(jnp.float32)
    top_v, top_i = lax.top_k(logits, TOP_K)
    gates = jax.nn.softmax(top_v, axis=-1)
    comb = jnp.sum(jax.nn.one_hot(top_i, N_EXPERTS, dtype=jnp.float32) * gates[..., None], axis=1)
    out = jnp.zeros_like(t)
    for e in range(N_EXPERTS):
        out = out + comb[:, e:e + 1].astype(t.dtype) * swiglu(t, wg[e], wu[e], wd[e])
    return out.reshape(B, S, D)


def setup_inputs(seed: int = 0) -> dict:
    key = jax.random.key(seed)
    ks = iter(jax.random.split(key, 40))

    def w(shape, fan_in):
        return jax.random.normal(next(ks), shape, jnp.float32) * (fan_in ** -0.5)

    def gain(shape):
        return 1.0 + 0.01 * jax.random.normal(next(ks), shape, jnp.float32)

    def small(shape, s=0.01):
        return s * jax.random.normal(next(ks), shape, jnp.float32)

    x = jax.random.normal(next(ks), (BATCH, SEQ, D_MODEL), jnp.float32)
    u = jax.random.uniform(next(ks), (N_LRU, D_RNN), jnp.float32, 0.9, 0.999)
    a0 = u ** (1.0 / LRU_C)
    lru_lambda = jnp.log(a0) - jnp.log1p(-a0)
    return {
        "x": x,
        "norm_mix": gain((DEPTH, D_MODEL)),
        "norm_ffn": gain((DEPTH, D_MODEL)),
        "norm_final": gain((D_MODEL,)),
        "rel_bias": small((NUM_BUCKETS, N_HEADS), 0.5),
        "attn_w_qkv": w((N_ATTN, D_MODEL, 3 * D_MODEL), D_MODEL),
        "attn_w_o": w((N_ATTN, N_HEADS * V_DIM, D_MODEL), N_HEADS * V_DIM),
        "attn_lambda_q1": small((N_ATTN, HEAD_DIM), 0.1),
        "attn_lambda_k1": small((N_ATTN, HEAD_DIM), 0.1),
        "attn_lambda_q2": small((N_ATTN, HEAD_DIM), 0.1),
        "attn_lambda_k2": small((N_ATTN, HEAD_DIM), 0.1),
        "attn_subln": gain((N_ATTN, V_DIM)),
        "lru_w_in": w((N_LRU, D_MODEL, 2 * D_RNN), D_MODEL),
        "lru_conv_w": w((N_LRU, CONV_WIDTH, D_RNN), CONV_WIDTH),
        "lru_conv_b": small((N_LRU, D_RNN)),
        "lru_gate_a_w": w((N_LRU, N_LRU_BLOCKS, LRU_BLOCK, LRU_BLOCK), LRU_BLOCK),
        "lru_gate_a_b": small((N_LRU, D_RNN)),
        "lru_gate_x_w": w((N_LRU, N_LRU_BLOCKS, LRU_BLOCK, LRU_BLOCK), LRU_BLOCK),
        "lru_gate_x_b": small((N_LRU, D_RNN)),
        "lru_lambda": lru_lambda,
        "lru_w_out": w((N_LRU, D_RNN, D_MODEL), D_RNN),
        "ffn_w_gate": w((N_DENSE, D_MODEL, D_FF), D_MODEL),
        "ffn_w_up": w((N_DENSE, D_MODEL, D_FF), D_MODEL),
        "ffn_w_down": w((N_DENSE, D_FF, D_MODEL), D_FF),
        "moe_router": w((N_MOE, D_MODEL, N_EXPERTS), D_MODEL),
        "moe_w_gate": w((N_MOE, N_EXPERTS, D_MODEL, D_FF_EXPERT), D_MODEL),
        "moe_w_up": w((N_MOE, N_EXPERTS, D_MODEL, D_FF_EXPERT), D_MODEL),
        "moe_w_down": w((N_MOE, N_EXPERTS, D_FF_EXPERT, D_MODEL), D_FF_EXPERT),
    }


def reference(x, norm_mix, norm_ffn, norm_final, rel_bias,
              attn_w_qkv, attn_w_o, attn_lambda_q1, attn_lambda_k1, attn_lambda_q2, attn_lambda_k2, attn_subln,
              lru_w_in, lru_conv_w, lru_conv_b, lru_gate_a_w, lru_gate_a_b, lru_gate_x_w, lru_gate_x_b,
              lru_lambda, lru_w_out,
              ffn_w_gate, ffn_w_up, ffn_w_down,
              moe_router, moe_w_gate, moe_w_up, moe_w_down):
    for i in range(DEPTH):
        h = rmsnorm(x, norm_mix[i])
        if i % 2 == 0:
            j = i // 2
            lam_init = 0.8 - 0.6 * math.exp(-0.3 * i)
            m = diff_attention(h, attn_w_qkv[j], attn_w_o[j], attn_lambda_q1[j], attn_lambda_k1[j],
                               attn_lambda_q2[j], attn_lambda_k2[j], attn_subln[j], rel_bias, lam_init)
        else:
            j = i // 2
            m = rglru_block(h, lru_w_in[j], lru_conv_w[j], lru_conv_b[j], lru_gate_a_w[j], lru_gate_a_b[j],
                            lru_gate_x_w[j], lru_gate_x_b[j], lru_lambda[j], lru_w_out[j])
        x = x + m
        h = rmsnorm(x, norm_ffn[i])
        if i % 2 == 0:
            j = i // 2
            f = swiglu(h, ffn_w_gate[j], ffn_w_up[j], ffn_w_down[j])
        else:
            j = i // 2
            f = moe_swiglu(h, moe_router[j], moe_w_gate[j], moe_w_up[j], moe_w_down[j])
        x = x + f
    return rmsnorm(x, norm_final)
```

```python
import functools
import math

import numpy as np
import jax
import jax.numpy as jnp
from jax import lax
from jax.experimental import pallas as pl
from jax.experimental.pallas import tpu as pltpu

F32 = jnp.float32
BF16 = jnp.bfloat16

N_HEADS = 8
HEAD_DIM = 64
V_DIM = 2 * HEAD_DIM
SCALE = HEAD_DIM ** -0.5
NEG_INF = -1e30
NUM_BUCKETS = 32
MAX_EXACT = NUM_BUCKETS // 2
MAX_DISTANCE = 128
N_LRU_BLOCKS = 10
LRU_BLOCK = 128
CONV_WIDTH = 4
LRU_C = 8.0
N_EXPERTS = 8
NORM_EPS = 1e-6
SUBLN_EPS = 1e-5
LAM_INIT_0 = 0.8 - 0.6 * math.exp(-0.3 * 0)

V7X_VMEM_BYTES = 64 * 1024 * 1024
SUBLANES = 8

ATTN_TILE = 512
TOKEN_TILE = 512
LRU_TILE = 256
ROUTE_TILE = 1024
EXPERT_ROW_TILE = 512
EXPERT_FF_CHUNKS = 2
DISPATCH_TILE = 512

NT_DIMS = (((1,), (1,)), ((), ()))


def _vmem_limit(nbytes):
    return int(min(nbytes, V7X_VMEM_BYTES - 6 * 1024 * 1024))


def _rms(x, g, eps):
    ms = jnp.mean(x * x, axis=-1, keepdims=True)
    return x * lax.rsqrt(ms + eps) * g


def _const_spec(shape):
    nd = len(shape)
    return pl.BlockSpec(shape, lambda *_: (0,) * nd, pipeline_mode=pl.Buffered(1))


def _qkv_kernel(x_ref, g_ref, wqT_ref, wk_ref, wvT_ref, qT_ref, k_ref, vT_ref):
    hn = _rms(x_ref[0], g_ref[...], NORM_EPS).astype(BF16)
    qT_ref[0] = lax.dot_general(wqT_ref[...], hn, NT_DIMS,
                                preferred_element_type=F32).astype(BF16)
    k_ref[0] = jnp.dot(hn, wk_ref[...], preferred_element_type=F32).astype(BF16)
    vT_ref[0, 0] = lax.dot_general(wvT_ref[...], hn, NT_DIMS,
                                   preferred_element_type=F32).astype(BF16)


def _qkv(x, g, wqT, wk, wvT, tm):
    B, S, D = x.shape
    n = S // tm
    return pl.pallas_call(
        _qkv_kernel,
        grid=(B, n),
        in_specs=[
            pl.BlockSpec((1, tm, D), lambda b, i: (b, i, 0)),
            _const_spec((1, D)),
            _const_spec((D, D)),
            _const_spec((D, D)),
            _const_spec((D, D)),
        ],
        out_specs=[
            pl.BlockSpec((1, D, tm), lambda b, i: (b, 0, i)),
            pl.BlockSpec((1, tm, D), lambda b, i: (b, i, 0)),
            pl.BlockSpec((1, 1, D, tm), lambda b, i: (b, i, 0, 0)),
        ],
        out_shape=[
            jax.ShapeDtypeStruct((B, D, S), BF16),
            jax.ShapeDtypeStruct((B, S, D), BF16),
            jax.ShapeDtypeStruct((B, n, D, tm), BF16),
        ],
        compiler_params=pltpu.CompilerParams(
            dimension_semantics=("arbitrary", "arbitrary"),
            vmem_limit_bytes=_vmem_limit(40 * 1024 * 1024)),
        name="qkv",
    )(x, g, wqT, wk, wvT)


def _t5_bucket_np(n):
    nf = np.maximum(n, 1).astype(np.float32)
    large = MAX_EXACT + (np.log(nf / np.float32(MAX_EXACT))
                         / np.float32(math.log(MAX_DISTANCE / MAX_EXACT))
                         * np.float32(NUM_BUCKETS - MAX_EXACT)).astype(np.int32)
    large = np.minimum(large, NUM_BUCKETS - 1)
    return np.where(n < MAX_EXACT, n, large)


def _bias_tiles(rel_bias, t):
    kk = np.arange(t)[:, None]
    qq = np.arange(t)[None, :]
    d_diag = qq - kk
    d_sub = d_diag + t
    assert t >= MAX_DISTANCE and _t5_bucket_np(np.array([t + 1]))[0] == NUM_BUCKETS - 1
    bk = np.stack([_t5_bucket_np(np.maximum(d_diag, 0)), _t5_bucket_np(d_sub)])
    rb = rel_bias.astype(F32) - rel_bias[NUM_BUCKETS - 1].astype(F32)[None, :]
    tiles = jnp.transpose(rb[bk], (3, 0, 1, 2))
    causal = jnp.asarray(np.stack([d_diag >= 0, np.ones_like(d_diag, bool)]))[None]
    return jnp.where(causal, tiles, NEG_INF)


def _attn_kernel(lqk_ref, qT_ref, k_ref, vT_ref, bias_ref, g_ref, o_ref,
                 q_scr, m_scr, l_scr, acc_scr, *, t):
    qi = pl.program_id(2)

    qT = qT_ref[0]
    row = lax.broadcasted_iota(jnp.int32, qT.shape, 0)
    zero = jnp.zeros_like(qT)
    q_scr[0] = jnp.where(row < HEAD_DIM, qT, zero)
    q_scr[1] = jnp.where(row >= HEAD_DIM, qT, zero)
    m_scr[...] = jnp.full(m_scr.shape, NEG_INF, F32)
    l_scr[...] = jnp.zeros(l_scr.shape, F32)
    acc_scr[...] = jnp.zeros(acc_scr.shape, F32)

    def process(j, bias_idx):
        kc = k_ref[0, pl.ds(pl.multiple_of(j * t, t), t), :]
        vt = vT_ref[0, j]
        for c in range(2):
            s = jnp.dot(kc, q_scr[c], preferred_element_type=F32)
            if bias_idx is not None:
                s = s + bias_ref[0, bias_idx]
            m_old = m_scr[c]
            m_new = jnp.maximum(m_old, jnp.max(s, axis=0, keepdims=True))
            alpha = jnp.exp(m_old - m_new)
            p = jnp.exp(s - m_new)
            l_scr[c] = alpha * l_scr[c] + jnp.sum(p, axis=0, keepdims=True)
            acc_scr[c] = alpha * acc_scr[c] + jnp.dot(
                vt, p.astype(BF16), preferred_element_type=F32)
            m_scr[c] = m_new

    def far_body(j, carry):
        process(j, None)
        return carry

    lax.fori_loop(0, jnp.maximum(qi - 1, 0), far_body, 0)

    @pl.when(qi >= 1)
    def _():
        process(qi - 1, 1)

    process(qi, 0)

    lqk = lqk_ref[...]
    e1 = jnp.exp(jnp.sum(lqk[0:1] * lqk[1:2], axis=-1, keepdims=True))
    e2 = jnp.exp(jnp.sum(lqk[2:3] * lqk[3:4], axis=-1, keepdims=True))
    lam = e1 - e2 + LAM_INIT_0
    o = acc_scr[0] / l_scr[0] - lam * (acc_scr[1] / l_scr[1])
    ms = jnp.mean(o * o, axis=0, keepdims=True)
    on = o * lax.rsqrt(ms + SUBLN_EPS) * g_ref[...] * (1.0 - LAM_INIT_0)
    o_ref[0] = on.T.astype(BF16)


def _attention(lqk, qT, k, vT, bias, g_col, t):
    B, D, S = qT.shape
    n = S // t
    H = D // V_DIM
    kern = functools.partial(_attn_kernel, t=t)
    return pl.pallas_call(
        kern,
        grid=(B, H, n),
        in_specs=[
            _const_spec(lqk.shape),
            pl.BlockSpec((1, V_DIM, t), lambda b, h, i: (b, h, i)),
            pl.BlockSpec((1, S, V_DIM), lambda b, h, i: (b, 0, h)),
            pl.BlockSpec((1, n, V_DIM, t), lambda b, h, i: (b, 0, h, 0)),
            pl.BlockSpec((1, 2, t, t), lambda b, h, i: (h, 0, 0, 0)),
            _const_spec((V_DIM, 1)),
        ],
        out_specs=pl.BlockSpec((1, t, V_DIM), lambda b, h, i: (b, i, h)),
        out_shape=jax.ShapeDtypeStruct((B, S, D), BF16),
        scratch_shapes=[
            pltpu.VMEM((2, V_DIM, t), BF16),
            pltpu.VMEM((2, 1, t), F32),
            pltpu.VMEM((2, 1, t), F32),
            pltpu.VMEM((2, V_DIM, t), F32),
        ],
        compiler_params=pltpu.CompilerParams(
            dimension_semantics=("arbitrary", "arbitrary", "arbitrary"),
            vmem_limit_bytes=_vmem_limit(48 * 1024 * 1024)),
        name="attn",
    )(lqk, qT, k, vT, bias, g_col)


def _ffn_kernel(o_ref, x_ref, wo_ref, g1_ref, wg_ref, wu_ref, wd_ref, g2_ref,
                x2_ref, h2_ref, *, n_chunks):
    x1 = x_ref[...] + jnp.dot(o_ref[...], wo_ref[...], preferred_element_type=F32)
    h1 = _rms(x1, g1_ref[...], NORM_EPS).astype(BF16)
    fc = wg_ref.shape[1] // n_chunks
    f = jnp.zeros_like(x1)
    for c in range(n_chunks):
        sl = slice(c * fc, (c + 1) * fc)
        gt = jnp.dot(h1, wg_ref[:, sl], preferred_element_type=F32)
        up = jnp.dot(h1, wu_ref[:, sl], preferred_element_type=F32)
        a = (gt * jax.nn.sigmoid(gt) * up).astype(BF16)
        f = f + jnp.dot(a, wd_ref[sl, :], preferred_element_type=F32)
    x2 = x1 + f
    x2_ref[...] = x2
    h2_ref[...] = _rms(x2, g2_ref[...], NORM_EPS).astype(BF16)


def _ffn(o, x, wo, g1, wg, wu, wd, g2, tm):
    T, D = x.shape
    Fd = wg.shape[1]
    n_chunks = 2 if (Fd // 2) % 128 == 0 else 1
    kern = functools.partial(_ffn_kernel, n_chunks=n_chunks)
    tile = lambda i: (i, 0)
    return pl.pallas_call(
        kern,
        grid=(T // tm,),
        in_specs=[
            pl.BlockSpec((tm, D), tile),
            pl.BlockSpec((tm, D), tile),
            _const_spec((D, D)),
            _const_spec((1, D)),
            _const_spec((D, Fd)),
            _const_spec((D, Fd)),
            _const_spec((Fd, D)),
            _const_spec((1, D)),
        ],
        out_specs=[pl.BlockSpec((tm, D), tile), pl.BlockSpec((tm, D), tile)],
        out_shape=[jax.ShapeDtypeStruct((T, D), F32), jax.ShapeDtypeStruct((T, D), BF16)],
        compiler_params=pltpu.CompilerParams(
            dimension_semantics=("arbitrary",),
            vmem_limit_bytes=_vmem_limit(56 * 1024 * 1024)),
        name="ffn",
    )(o, x, wo, g1, wg, wu, wd, g2)


def _gelu_tanh(x):
    return 0.5 * x * (1.0 + jnp.tanh(math.sqrt(2.0 / math.pi) * (x + 0.044715 * (x * x * x))))


def _lru_kernel(h2_ref, x2_ref, win_ref, cw_ref, cb_ref, gaw_ref, gab_ref, gxw_ref, gxb_ref,
                lam_ref, wout_ref, gn_ref, rT_ref, x3_ref, lg_ref,
                xbuf, a_scr, b_scr, h_scr, hcar, *, ts, dr):
    si = pl.program_id(1)
    halo = SUBLANES

    @pl.when(si == 0)
    def _():
        xbuf[0:halo, :] = jnp.zeros((halo, dr), F32)
        hcar[...] = jnp.zeros(hcar.shape, F32)

    gx = jnp.dot(h2_ref[0], win_ref[...], preferred_element_type=F32)
    gate = _gelu_tanh(gx[:, :dr])
    xb = gx[:, dr:]
    xbuf[halo:halo + ts, :] = xb

    cw = cw_ref[...]
    xc = cb_ref[...] + cw[CONV_WIDTH - 1:CONV_WIDTH] * xb
    for j in range(CONV_WIDTH - 1):
        back = CONV_WIDTH - 1 - j
        xc = xc + cw[j:j + 1] * xbuf[halo - back:halo - back + ts, :]
    xbuf[0:halo, :] = xb[ts - halo:ts, :]

    xcb = xc.astype(BF16)
    r_parts, i_parts = [], []
    for n in range(dr // LRU_BLOCK):
        blk = xcb[:, n * LRU_BLOCK:(n + 1) * LRU_BLOCK]
        r_parts.append(jnp.dot(blk, gaw_ref[n], preferred_element_type=F32))
        i_parts.append(jnp.dot(blk, gxw_ref[n], preferred_element_type=F32))
    r = jax.nn.sigmoid(jnp.concatenate(r_parts, axis=1) + gab_ref[...])
    ig = jax.nn.sigmoid(jnp.concatenate(i_parts, axis=1) + gxb_ref[...])
    z = -lam_ref[...]
    ez = jnp.exp(-jnp.abs(z))
    u = 1.0 + ez
    du = u - 1.0
    log1p_ez = jnp.where(du == 0.0, ez, jnp.log(u) * (ez / du))
    softplus = jnp.maximum(z, 0.0) + log1p_ez
    log_a = (-LRU_C) * r * softplus
    a = jnp.exp(log_a)
    th = jnp.tanh(log_a)
    b = jnp.sqrt(-2.0 * th / (1.0 - th)) * (ig * xc)

    ng = ts // SUBLANES
    a3 = a.reshape(ng, SUBLANES, dr)
    b3 = b.reshape(ng, SUBLANES, dr)
    sub = lax.broadcasted_iota(jnp.int32, a3.shape, 1)
    d = 1
    while d < SUBLANES:
        keep = sub >= d
        a_s = pltpu.roll(a3, d, axis=1)
        b_s = pltpu.roll(b3, d, axis=1)
        b3 = jnp.where(keep, a3 * b_s + b3, b3)
        a3 = jnp.where(keep, a3 * a_s, a3)
        d *= 2
    a_scr[...] = a3.reshape(ts, dr)
    b_scr[...] = b3.reshape(ts, dr)

    def group(g, hprev):
        r0 = pl.multiple_of(g * SUBLANES, SUBLANES)
        hg = a_scr[pl.ds(r0, SUBLANES), :] * hprev + b_scr[pl.ds(r0, SUBLANES), :]
        h_scr[pl.ds(r0, SUBLANES), :] = hg
        return jnp.broadcast_to(hg[SUBLANES - 1:SUBLANES, :], (SUBLANES, dr))

    hcar[...] = lax.fori_loop(0, ng, group, hcar[...], unroll=4)

    y = (h_scr[...] * gate).astype(BF16)
    x3 = x2_ref[0] + jnp.dot(y, wout_ref[...], preferred_element_type=F32)
    x3_ref[0] = x3

    h3 = _rms(x3, gn_ref[...], NORM_EPS)
    h_hi = h3.astype(BF16)
    h_lo = (h3 - h_hi.astype(F32)).astype(BF16)
    rT = rT_ref[...]
    r_hi = rT.astype(BF16)
    r_lo = (rT - r_hi.astype(F32)).astype(BF16)
    lg = lax.dot_general(r_hi, h_hi, NT_DIMS, preferred_element_type=F32)
    lg = lg + lax.dot_general(r_lo, h_hi, NT_DIMS, preferred_element_type=F32)
    lg = lg + lax.dot_general(r_hi, h_lo, NT_DIMS, preferred_element_type=F32)
    lg_ref[0] = lg


def _lru(h2, x2, win, cw, cb, gaw, gab, gxw, gxb, lam, wout, gn, rT, ts):
    B, S, D = x2.shape
    dr = wout.shape[0]
    E = rT.shape[0]
    kern = functools.partial(_lru_kernel, ts=ts, dr=dr)
    tile = lambda b, i: (b, i, 0)
    return pl.pallas_call(
        kern,
        grid=(B, S // ts),
        in_specs=[
            pl.BlockSpec((1, ts, D), tile),
            pl.BlockSpec((1, ts, D), tile),
            _const_spec(win.shape),
            _const_spec(cw.shape),
            _const_spec(cb.shape),
            _const_spec(gaw.shape),
            _const_spec(gab.shape),
            _const_spec(gxw.shape),
            _const_spec(gxb.shape),
            _const_spec(lam.shape),
            _const_spec(wout.shape),
            _const_spec(gn.shape),
            _const_spec(rT.shape),
        ],
        out_specs=[pl.BlockSpec((1, ts, D), tile),
                   pl.BlockSpec((1, E, ts), lambda b, i: (b, 0, i))],
        out_shape=[jax.ShapeDtypeStruct((B, S, D), F32),
                   jax.ShapeDtypeStruct((B, E, S), F32)],
        scratch_shapes=[
            pltpu.VMEM((SUBLANES + ts, dr), F32),
            pltpu.VMEM((ts, dr), F32),
            pltpu.VMEM((ts, dr), F32),
            pltpu.VMEM((ts, dr), F32),
            pltpu.VMEM((SUBLANES, dr), F32),
        ],
        compiler_params=pltpu.CompilerParams(
            dimension_semantics=("arbitrary", "arbitrary"),
            vmem_limit_bytes=_vmem_limit(48 * 1024 * 1024)),
        name="lru",
    )(h2, x2, win, cw, cb, gaw, gab, gxw, gxb, lam, wout, gn, rT)


def _route_kernel(lg_ref, idx_ref, gate_ref, cnt_ref, carry, *, tr):
    @pl.when(pl.program_id(0) == 0)
    def _():
        carry[...] = jnp.zeros(carry.shape, F32)

    lg = lg_ref[...]
    E = lg.shape[0]
    eidx = lax.broadcasted_iota(jnp.int32, lg.shape, 0).astype(F32)
    m1 = jnp.max(lg, axis=0, keepdims=True)
    i1 = jnp.min(jnp.where(lg == m1, eidx, float(E)), axis=0, keepdims=True)
    oh1 = eidx == i1
    lg2 = jnp.where(oh1, -jnp.inf, lg)
    m2 = jnp.max(lg2, axis=0, keepdims=True)
    i2 = jnp.min(jnp.where(lg2 == m2, eidx, float(E)), axis=0, keepdims=True)
    oh2 = eidx == i2
    ex = jnp.exp(m2 - m1)
    g1 = 1.0 / (1.0 + ex)
    g2 = ex / (1.0 + ex)

    both = jnp.where(oh1 | oh2, 1.0, 0.0).astype(BF16)
    ti = lax.broadcasted_iota(jnp.int32, (tr, tr), 0)
    tj = lax.broadcasted_iota(jnp.int32, (tr, tr), 1)
    before = jnp.where(ti < tj, 1.0, 0.0).astype(BF16)
    rank = jnp.dot(both, before, preferred_element_type=F32) + carry[...][:, 0:1]
    r1 = jnp.sum(jnp.where(oh1, rank, 0.0), axis=0, keepdims=True)
    r2 = jnp.sum(jnp.where(oh2, rank, 0.0), axis=0, keepdims=True)
    tot = carry[...] + jnp.sum(both.astype(F32), axis=1, keepdims=True)
    carry[...] = tot
    cnt_ref[...] = tot.astype(jnp.int32)

    orow = lax.broadcasted_iota(jnp.int32, (8, tr), 0)
    packed = jnp.where(orow == 0, i1, jnp.where(orow == 1, i2, jnp.where(orow == 2, r1, r2)))
    idx_ref[...] = packed.astype(jnp.int32)
    gate_ref[...] = jnp.where(orow == 0, g1, g2)


def _route(lgT, tr):
    E, T = lgT.shape
    kern = functools.partial(_route_kernel, tr=tr)
    return pl.pallas_call(
        kern,
        grid=(T // tr,),
        in_specs=[pl.BlockSpec((E, tr), lambda i: (0, i))],
        out_specs=[pl.BlockSpec((8, tr), lambda i: (0, i)),
                   pl.BlockSpec((8, tr), lambda i: (0, i)),
                   pl.BlockSpec((E, 128), lambda i: (0, 0))],
        out_shape=[jax.ShapeDtypeStruct((8, T), jnp.int32),
                   jax.ShapeDtypeStruct((8, T), F32),
                   jax.ShapeDtypeStruct((E, 128), jnp.int32)],
        scratch_shapes=[pltpu.VMEM((E, 128), F32)],
        compiler_params=pltpu.CompilerParams(dimension_semantics=("arbitrary",)),
        name="route",
    )(lgT)


def _row_copy(src_hbm, dst_hbm, s, d, sem):
    return pltpu.make_async_copy(src_hbm.at[pl.ds(s, 1)], dst_hbm.at[pl.ds(d, 1)], sem)


def _dispatch_kernel(pos_ref, x_hbm, xs_in, xs_out, sem, *, td):
    del xs_in
    base = pl.program_id(0) * td

    def issue(t, c):
        for kk in range(2):
            _row_copy(x_hbm, xs_out, base + t, pos_ref[0, kk, t], sem).start()
        return c

    lax.fori_loop(0, td, issue, 0, unroll=8)

    def drain(t, c):
        for kk in range(2):
            _row_copy(x_hbm, xs_out, 0, 0, sem).wait()
        return c

    lax.fori_loop(0, td, drain, 0, unroll=8)


def _dispatch(pos3, x, xs_zero, td):
    T, D = x.shape
    kern = functools.partial(_dispatch_kernel, td=td)
    return pl.pallas_call(
        kern,
        grid=(T // td,),
        in_specs=[
            pl.BlockSpec((1, 2, td), lambda i: (i, 0, 0), memory_space=pltpu.SMEM),
            pl.BlockSpec(memory_space=pl.ANY),
            pl.BlockSpec(memory_space=pl.ANY),
        ],
        out_specs=pl.BlockSpec(memory_space=pl.ANY),
        out_shape=jax.ShapeDtypeStruct(xs_zero.shape, xs_zero.dtype),
        scratch_shapes=[pltpu.SemaphoreType.DMA],
        input_output_aliases={2: 0},
        compiler_params=pltpu.CompilerParams(
            dimension_semantics=("arbitrary",), has_side_effects=True),
        name="dispatch",
    )(pos3, x, xs_zero)


def _expert_kernel(te_ref, nu_ref, xs_ref, gn_ref, wg_ref, wu_ref, wd_ref, ys_ref,
                   h_scr, acc_scr, *, nf):
    i = pl.program_id(0)
    j = pl.program_id(1)

    @pl.when(i < nu_ref[0])
    def _():
        @pl.when(j == 0)
        def _():
            h_scr[...] = _rms(xs_ref[...], gn_ref[...], NORM_EPS).astype(BF16)
            acc_scr[...] = jnp.zeros(acc_scr.shape, F32)

        h = h_scr[...]
        gt = jnp.dot(h, wg_ref[0], preferred_element_type=F32)
        up = jnp.dot(h, wu_ref[0], preferred_element_type=F32)
        a = (gt * jax.nn.sigmoid(gt) * up).astype(BF16)
        acc_scr[...] += jnp.dot(a, wd_ref[0], preferred_element_type=F32)

        @pl.when(j == nf - 1)
        def _():
            ys_ref[...] = acc_scr[...]

    @pl.when((i >= nu_ref[0]) & (j == nf - 1))
    def _():
        ys_ref[...] = jnp.zeros(ys_ref.shape, F32)


def _experts(tile_expert, n_used, xs, gn, wg, wu, wd, tm, nf):
    NP, D = xs.shape
    E, _, Fd = wg.shape
    fc = Fd // nf
    nt = NP // tm

    def eff(i, nu):
        return jnp.minimum(i, nu[0] - 1)

    def chunk(i, j, nu):
        ie = eff(i, nu)
        je = jnp.where(i < nu[0], j, nf - 1)
        return jnp.where(ie % 2 == 0, je, nf - 1 - je)

    kern = functools.partial(_expert_kernel, nf=nf)
    return pl.pallas_call(
        kern,
        grid_spec=pltpu.PrefetchScalarGridSpec(
            num_scalar_prefetch=2,
            grid=(nt, nf),
            in_specs=[
                pl.BlockSpec((tm, D), lambda i, j, te, nu: (eff(i, nu), 0)),
                pl.BlockSpec((1, D), lambda i, j, te, nu: (0, 0)),
                pl.BlockSpec((1, D, fc), lambda i, j, te, nu: (te[eff(i, nu)], 0, chunk(i, j, nu))),
                pl.BlockSpec((1, D, fc), lambda i, j, te, nu: (te[eff(i, nu)], 0, chunk(i, j, nu))),
                pl.BlockSpec((1, fc, D), lambda i, j, te, nu: (te[eff(i, nu)], chunk(i, j, nu), 0)),
            ],
            out_specs=pl.BlockSpec((tm, D), lambda i, j, te, nu: (i, 0)),
            scratch_shapes=[pltpu.VMEM((tm, D), BF16), pltpu.VMEM((tm, D), F32)],
        ),
        out_shape=jax.ShapeDtypeStruct((NP, D), F32),
        compiler_params=pltpu.CompilerParams(
            dimension_semantics=("arbitrary", "arbitrary"),
            vmem_limit_bytes=_vmem_limit(56 * 1024 * 1024)),
        name="experts",
    )(tile_expert, n_used, xs, gn, wg, wu, wd)


def _combine_kernel(pos_ref, gate_ref, x_ref, gn_ref, ys_hbm, out_ref, buf, sem, *, tc):
    def issue(t, c):
        for kk in range(2):
            pltpu.make_async_copy(ys_hbm.at[pl.ds(pos_ref[0, kk, t], 1)],
                                  buf.at[kk, pl.ds(t, 1)], sem).start()
        return c

    lax.fori_loop(0, tc, issue, 0, unroll=8)

    def drain(t, c):
        for kk in range(2):
            pltpu.make_async_copy(ys_hbm.at[pl.ds(0, 1)], buf.at[0, pl.ds(0, 1)], sem).wait()
        return c

    lax.fori_loop(0, tc, drain, 0, unroll=8)

    g = gate_ref[...]
    x4 = x_ref[...] + g[:, 0:1] * buf[0] + g[:, 1:2] * buf[1]
    out_ref[...] = _rms(x4, gn_ref[...], NORM_EPS)


def _combine(pos3, gates_col, x, gn, ys, tc):
    T, D = x.shape
    kern = functools.partial(_combine_kernel, tc=tc)
    return pl.pallas_call(
        kern,
        grid=(T // tc,),
        in_specs=[
            pl.BlockSpec((1, 2, tc), lambda i: (i, 0, 0), memory_space=pltpu.SMEM),
            pl.BlockSpec((tc, 2), lambda i: (i, 0)),
            pl.BlockSpec((tc, D), lambda i: (i, 0)),
            _const_spec((1, D)),
            pl.BlockSpec(memory_space=pl.ANY),
        ],
        out_specs=pl.BlockSpec((tc, D), lambda i: (i, 0)),
        out_shape=jax.ShapeDtypeStruct((T, D), F32),
        scratch_shapes=[pltpu.VMEM((2, tc, D), F32), pltpu.SemaphoreType.DMA],
        compiler_params=pltpu.CompilerParams(
            dimension_semantics=("arbitrary",),
            vmem_limit_bytes=_vmem_limit(40 * 1024 * 1024)),
        name="combine",
    )(pos3, gates_col, x, gn, ys)


def _tile(n, pref):
    t = min(n, pref)
    assert n % t == 0, (n, t)
    return t


def kernel(x, norm_mix, norm_ffn, norm_final, rel_bias, attn_w_qkv, attn_w_o, attn_lambda_q1, attn_lambda_k1, attn_lambda_q2, attn_lambda_k2, attn_subln, lru_w_in, lru_conv_w, lru_conv_b, lru_gate_a_w, lru_gate_a_b, lru_gate_x_w, lru_gate_x_b, lru_lambda, lru_w_out, ffn_w_gate, ffn_w_up, ffn_w_down, moe_router, moe_w_gate, moe_w_up, moe_w_down):
    B, S, D = x.shape
    T = B * S
    assert D == N_HEADS * V_DIM
    row = lambda v: v.reshape(1, -1).astype(F32)

    ta = _tile(S, ATTN_TILE)
    wqkv = attn_w_qkv[0]
    wqT = (wqkv[:, :D] * SCALE).T.astype(BF16)
    wk = wqkv[:, D:2 * D].astype(BF16)
    wvT = wqkv[:, 2 * D:].T.astype(BF16)
    qT, k, vT = _qkv(x, row(norm_mix[0]), wqT, wk, wvT, ta)
    lqk = jnp.stack([attn_lambda_q1[0], attn_lambda_k1[0],
                     attn_lambda_q2[0], attn_lambda_k2[0]]).astype(F32)
    bias = _bias_tiles(rel_bias, ta)
    o = _attention(lqk, qT, k, vT, bias, attn_subln[0].reshape(V_DIM, 1).astype(F32), ta)

    tm = _tile(T, TOKEN_TILE)
    x2, h2 = _ffn(o.reshape(T, D), x.reshape(T, D), attn_w_o[0].astype(BF16), row(norm_ffn[0]),
                  ffn_w_gate[0].astype(BF16), ffn_w_up[0].astype(BF16),
                  ffn_w_down[0].astype(BF16), row(norm_mix[1]), tm)

    ts = _tile(S, LRU_TILE)
    x3, lg = _lru(h2.reshape(B, S, D), x2.reshape(B, S, D), lru_w_in[0].astype(BF16),
                  lru_conv_w[0].astype(F32), row(lru_conv_b[0]),
                  lru_gate_a_w[0].astype(BF16), row(lru_gate_a_b[0]),
                  lru_gate_x_w[0].astype(BF16), row(lru_gate_x_b[0]),
                  row(lru_lambda[0]), lru_w_out[0].astype(BF16), row(norm_ffn[1]),
                  moe_router[0].T.astype(F32), ts)
    x3 = x3.reshape(T, D)
    E = moe_router.shape[-1]
    lgT = jnp.transpose(lg, (1, 0, 2)).reshape(E, T)

    idx, gates, counts = _route(lgT, _tile(T, ROUTE_TILE))
    rm = EXPERT_ROW_TILE
    cnt = counts[:, 0]
    padded = ((cnt + rm - 1) // rm) * rm
    ends = jnp.cumsum(padded)
    starts = ends - padded
    pos = jnp.stack([starts[idx[0]] + idx[2], starts[idx[1]] + idx[3]])
    n_tiles = (2 * T) // rm + E
    tile_expert = jnp.minimum(
        jnp.searchsorted(ends, jnp.arange(n_tiles, dtype=jnp.int32) * rm, side="right"),
        E - 1).astype(jnp.int32)
    n_used = (ends[-1] // rm).astype(jnp.int32).reshape(1)

    td = _tile(T, DISPATCH_TILE)
    pos3 = pos.reshape(2, T // td, td).transpose(1, 0, 2)
    xs = _dispatch(pos3, x3, jnp.zeros((n_tiles * rm, D), F32), td)
    ys = _experts(tile_expert, n_used, xs, row(norm_ffn[1]), moe_w_gate[0].astype(BF16),
                  moe_w_up[0].astype(BF16), moe_w_down[0].astype(BF16), rm, EXPERT_FF_CHUNKS)
    out = _combine(pos3, gates[:2].T, x3, row(norm_final), ys, td)
    return out.reshape(B, S, D)
```

```python
import functools
import math

import numpy as np
import jax
import jax.numpy as jnp
from jax import lax
from jax.experimental import pallas as pl
from jax.experimental.pallas import tpu as pltpu

F32 = jnp.float32
BF16 = jnp.bfloat16

N_HEADS = 8
HEAD_DIM = 64
V_DIM = 2 * HEAD_DIM
SCALE = HEAD_DIM ** -0.5
LOG2E = math.log2(math.e)
NEG_INF = -1e30
NUM_BUCKETS = 32
MAX_EXACT = NUM_BUCKETS // 2
MAX_DISTANCE = 128
N_LRU_BLOCKS = 10
LRU_BLOCK = 128
CONV_WIDTH = 4
LRU_C = 8.0
N_EXPERTS = 8
NORM_EPS = 1e-6
SUBLN_EPS = 1e-5
LAM_INIT_0 = 0.8 - 0.6 * math.exp(-0.3 * 0)

V7X_VMEM_BYTES = 64 * 1024 * 1024
SUBLANES = 8

ATTN_TILE = 512
TOKEN_TILE = 512
LRU_TILE = 256
ROUTE_TILE = 1024
EXPERT_ROW_TILE = 512
EXPERT_FF_CHUNKS = 2
DISPATCH_TILE = 512

NT_DIMS = (((1,), (1,)), ((), ()))


def _vmem_limit(nbytes):
    return int(min(nbytes, V7X_VMEM_BYTES - 6 * 1024 * 1024))


def _rms(x, g, eps):
    ms = jnp.mean(x * x, axis=-1, keepdims=True)
    return x * lax.rsqrt(ms + eps) * g


def _const_spec(shape):
    nd = len(shape)
    return pl.BlockSpec(shape, lambda *_: (0,) * nd, pipeline_mode=pl.Buffered(1))


def _qkv_kernel(x_ref, g_ref, wqT_ref, wk_ref, wvT_ref, qT_ref, k_ref, vT_ref):
    hn = _rms(x_ref[0], g_ref[...], NORM_EPS).astype(BF16)
    qT_ref[0] = lax.dot_general(wqT_ref[...], hn, NT_DIMS,
                                preferred_element_type=F32).astype(BF16)
    k_ref[0] = jnp.dot(hn, wk_ref[...], preferred_element_type=F32).astype(BF16)
    vT_ref[0, 0] = lax.dot_general(wvT_ref[...], hn, NT_DIMS,
                                   preferred_element_type=F32).astype(BF16)


def _qkv(x, g, wqT, wk, wvT, tm):
    B, S, D = x.shape
    n = S // tm
    return pl.pallas_call(
        _qkv_kernel,
        grid=(B, n),
        in_specs=[
            pl.BlockSpec((1, tm, D), lambda b, i: (b, i, 0)),
            _const_spec((1, D)),
            _const_spec((D, D)),
            _const_spec((D, D)),
            _const_spec((D, D)),
        ],
        out_specs=[
            pl.BlockSpec((1, D, tm), lambda b, i: (b, 0, i)),
            pl.BlockSpec((1, tm, D), lambda b, i: (b, i, 0)),
            pl.BlockSpec((1, 1, D, tm), lambda b, i: (b, i, 0, 0)),
        ],
        out_shape=[
            jax.ShapeDtypeStruct((B, D, S), BF16),
            jax.ShapeDtypeStruct((B, S, D), BF16),
            jax.ShapeDtypeStruct((B, n, D, tm), BF16),
        ],
        compiler_params=pltpu.CompilerParams(
            dimension_semantics=("arbitrary", "arbitrary"),
            vmem_limit_bytes=_vmem_limit(40 * 1024 * 1024)),
        name="qkv",
    )(x, g, wqT, wk, wvT)


def _t5_bucket_np(n):
    nf = np.maximum(n, 1).astype(np.float32)
    large = MAX_EXACT + (np.log(nf / np.float32(MAX_EXACT))
                         / np.float32(math.log(MAX_DISTANCE / MAX_EXACT))
                         * np.float32(NUM_BUCKETS - MAX_EXACT)).astype(np.int32)
    large = np.minimum(large, NUM_BUCKETS - 1)
    return np.where(n < MAX_EXACT, n, large)


def _bias_kernel(w_ref, out_ref, *, t):
    full = jnp.broadcast_to(w_ref[0], (t, 4 * t))
    c = pltpu.roll(full, 0, axis=1, stride=1, stride_axis=0)
    out_ref[0, 0] = c[:, :t]
    out_ref[0, 1] = c[:, t:2 * t]


def _bias_tiles(rel_bias, t):
    assert t >= MAX_DISTANCE and _t5_bucket_np(np.array([t + 1]))[0] == NUM_BUCKETS - 1
    H = rel_bias.shape[1]
    rb = rel_bias.astype(F32) - rel_bias[NUM_BUCKETS - 1].astype(F32)[None, :]
    near = jnp.take(rb, jnp.asarray(_t5_bucket_np(np.arange(2 * t))), axis=0).T
    w = jnp.concatenate([near, jnp.full((H, 2 * t), NEG_INF, F32)], axis=1) * LOG2E
    return pl.pallas_call(
        functools.partial(_bias_kernel, t=t),
        grid=(H,),
        in_specs=[pl.BlockSpec((1, 1, 4 * t), lambda h: (h, 0, 0))],
        out_specs=pl.BlockSpec((1, 2, t, t), lambda h: (h, 0, 0, 0)),
        out_shape=jax.ShapeDtypeStruct((H, 2, t, t), F32),
        compiler_params=pltpu.CompilerParams(dimension_semantics=("arbitrary",)),
        name="bias",
    )(w.reshape(H, 1, 4 * t))


def _attn_kernel(lqk_ref, qT_ref, k_ref, vT_ref, bias_ref, g_ref, o_ref,
                 q_scr, sa_scr, sb_scr, mba_scr, mbb_scr, m_scr, l_scr, acc_scr, *, t):
    qi = pl.program_id(2)

    qT = qT_ref[0]
    row = lax.broadcasted_iota(jnp.int32, qT.shape, 0)
    zero = jnp.zeros_like(qT)
    q_scr[0] = jnp.where(row < HEAD_DIM, qT, zero)
    q_scr[1] = jnp.where(row >= HEAD_DIM, qT, zero)
    m_scr[...] = jnp.full(m_scr.shape, NEG_INF, F32)
    l_scr[...] = jnp.zeros(l_scr.shape, F32)
    acc_scr[...] = jnp.zeros(acc_scr.shape, F32)

    def scores(j, bias_idx, s_scr, mb_scr):
        kc = k_ref[0, pl.ds(pl.multiple_of(j * t, t), t), :]
        for c in range(2):
            s = jnp.dot(kc, q_scr[c], preferred_element_type=F32)
            if bias_idx is not None:
                s = s + bias_ref[0, bias_idx]
            s_scr[c] = s
            mb_scr[c] = jnp.max(s, axis=0, keepdims=True)

    def consume(j, s_scr, mb_scr):
        vt = vT_ref[0, j]
        for c in range(2):
            m_old = m_scr[c]
            m_new = jnp.maximum(m_old, mb_scr[c])
            alpha = jnp.exp2(m_old - m_new)
            p = jnp.exp2(s_scr[c] - m_new)
            l_scr[c] = alpha * l_scr[c] + jnp.sum(p, axis=0, keepdims=True)
            acc_scr[c] = alpha * acc_scr[c] + jnp.dot(
                vt, p.astype(BF16), preferred_element_type=F32)
            m_scr[c] = m_new

    a_bufs = (sa_scr, mba_scr)
    b_bufs = (sb_scr, mbb_scr)
    scores(qi, 0, *a_bufs)

    @pl.when(qi == 0)
    def _():
        consume(qi, *a_bufs)

    @pl.when(qi >= 1)
    def _():
        scores(qi - 1, 1, *b_bufs)
        consume(qi, *a_bufs)
        n_far = qi - 1

        def pair(pp, carry):
            j0 = qi - 2 - 2 * pp
            scores(j0, None, *a_bufs)
            consume(j0 + 1, *b_bufs)
            scores(j0 - 1, None, *b_bufs)
            consume(j0, *a_bufs)
            return carry

        lax.fori_loop(0, n_far // 2, pair, 0)

        @pl.when(n_far % 2 == 1)
        def _():
            scores(0, None, *a_bufs)
            consume(1, *b_bufs)
            consume(0, *a_bufs)

        @pl.when(n_far % 2 == 0)
        def _():
            consume(0, *b_bufs)

    lqk = lqk_ref[...]
    e1 = jnp.exp(jnp.sum(lqk[0:1] * lqk[1:2], axis=-1, keepdims=True))
    e2 = jnp.exp(jnp.sum(lqk[2:3] * lqk[3:4], axis=-1, keepdims=True))
    lam = e1 - e2 + LAM_INIT_0
    o = acc_scr[0] / l_scr[0] - lam * (acc_scr[1] / l_scr[1])
    ms = jnp.mean(o * o, axis=0, keepdims=True)
    on = o * lax.rsqrt(ms + SUBLN_EPS) * g_ref[...] * (1.0 - LAM_INIT_0)
    o_ref[0] = on.T.astype(BF16)


def _attention(lqk, qT, k, vT, bias, g_col, t):
    B, D, S = qT.shape
    n = S // t
    H = D // V_DIM
    kern = functools.partial(_attn_kernel, t=t)
    return pl.pallas_call(
        kern,
        grid=(B, H, n),
        in_specs=[
            _const_spec(lqk.shape),
            pl.BlockSpec((1, V_DIM, t), lambda b, h, i: (b, h, i)),
            pl.BlockSpec((1, S, V_DIM), lambda b, h, i: (b, 0, h)),
            pl.BlockSpec((1, n, V_DIM, t), lambda b, h, i: (b, 0, h, 0)),
            pl.BlockSpec((1, 2, t, t), lambda b, h, i: (h, 0, 0, 0)),
            _const_spec((V_DIM, 1)),
        ],
        out_specs=pl.BlockSpec((1, t, V_DIM), lambda b, h, i: (b, i, h)),
        out_shape=jax.ShapeDtypeStruct((B, S, D), BF16),
        scratch_shapes=[
            pltpu.VMEM((2, V_DIM, t), BF16),
            pltpu.VMEM((2, t, t), F32),
            pltpu.VMEM((2, t, t), F32),
            pltpu.VMEM((2, 1, t), F32),
            pltpu.VMEM((2, 1, t), F32),
            pltpu.VMEM((2, 1, t), F32),
            pltpu.VMEM((2, 1, t), F32),
            pltpu.VMEM((2, V_DIM, t), F32),
        ],
        compiler_params=pltpu.CompilerParams(
            dimension_semantics=("arbitrary", "arbitrary", "arbitrary"),
            vmem_limit_bytes=_vmem_limit(48 * 1024 * 1024)),
        name="attn",
    )(lqk, qT, k, vT, bias, g_col)


def _ffn_kernel(o_ref, x_ref, wo_ref, g1_ref, wg_ref, wu_ref, wd_ref, g2_ref,
                x2_ref, h2_ref, *, n_chunks):
    x1 = x_ref[...] + jnp.dot(o_ref[...], wo_ref[...], preferred_element_type=F32)
    h1 = _rms(x1, g1_ref[...], NORM_EPS).astype(BF16)
    fc = wg_ref.shape[1] // n_chunks
    f = jnp.zeros_like(x1)
    for c in range(n_chunks):
        sl = slice(c * fc, (c + 1) * fc)
        gt = jnp.dot(h1, wg_ref[:, sl], preferred_element_type=F32)
        up = jnp.dot(h1, wu_ref[:, sl], preferred_element_type=F32)
        a = (gt * jax.nn.sigmoid(gt) * up).astype(BF16)
        f = f + jnp.dot(a, wd_ref[sl, :], preferred_element_type=F32)
    x2 = x1 + f
    x2_ref[...] = x2
    h2_ref[...] = _rms(x2, g2_ref[...], NORM_EPS).astype(BF16)


def _ffn(o, x, wo, g1, wg, wu, wd, g2, tm):
    T, D = x.shape
    Fd = wg.shape[1]
    n_chunks = 2 if (Fd // 2) % 128 == 0 else 1
    kern = functools.partial(_ffn_kernel, n_chunks=n_chunks)
    tile = lambda i: (i, 0)
    return pl.pallas_call(
        kern,
        grid=(T // tm,),
        in_specs=[
            pl.BlockSpec((tm, D), tile),
            pl.BlockSpec((tm, D), tile),
            _const_spec((D, D)),
            _const_spec((1, D)),
            _const_spec((D, Fd)),
            _const_spec((D, Fd)),
            _const_spec((Fd, D)),
            _const_spec((1, D)),
        ],
        out_specs=[pl.BlockSpec((tm, D), tile), pl.BlockSpec((tm, D), tile)],
        out_shape=[jax.ShapeDtypeStruct((T, D), F32), jax.ShapeDtypeStruct((T, D), BF16)],
        compiler_params=pltpu.CompilerParams(
            dimension_semantics=("arbitrary",),
            vmem_limit_bytes=_vmem_limit(56 * 1024 * 1024)),
        name="ffn",
    )(o, x, wo, g1, wg, wu, wd, g2)


def _gelu_tanh(x):
    return 0.5 * x * (1.0 + jnp.tanh(math.sqrt(2.0 / math.pi) * (x + 0.044715 * (x * x * x))))


def _lru_kernel(h2_ref, x2_ref, win_ref, cw_ref, cb_ref, gaw_ref, gab_ref, gxw_ref, gxb_ref,
                lam_ref, wout_ref, gn_ref, rT_ref, x3_ref, lg_ref,
                xbuf, a_scr, b_scr, h_scr, hcar, *, ts, dr):
    si = pl.program_id(1)
    halo = SUBLANES

    @pl.when(si == 0)
    def _():
        xbuf[0:halo, :] = jnp.zeros((halo, dr), F32)
        hcar[...] = jnp.zeros(hcar.shape, F32)

    gx = jnp.dot(h2_ref[0], win_ref[...], preferred_element_type=F32)
    gate = _gelu_tanh(gx[:, :dr])
    xb = gx[:, dr:]
    xbuf[halo:halo + ts, :] = xb

    cw = cw_ref[...]
    xc = cb_ref[...] + cw[CONV_WIDTH - 1:CONV_WIDTH] * xb
    for j in range(CONV_WIDTH - 1):
        back = CONV_WIDTH - 1 - j
        xc = xc + cw[j:j + 1] * xbuf[halo - back:halo - back + ts, :]
    xbuf[0:halo, :] = xb[ts - halo:ts, :]

    xcb = xc.astype(BF16)
    r_parts, i_parts = [], []
    for n in range(dr // LRU_BLOCK):
        blk = xcb[:, n * LRU_BLOCK:(n + 1) * LRU_BLOCK]
        r_parts.append(jnp.dot(blk, gaw_ref[n], preferred_element_type=F32))
        i_parts.append(jnp.dot(blk, gxw_ref[n], preferred_element_type=F32))
    r = jax.nn.sigmoid(jnp.concatenate(r_parts, axis=1) + gab_ref[...])
    ig = jax.nn.sigmoid(jnp.concatenate(i_parts, axis=1) + gxb_ref[...])
    z = -lam_ref[...]
    ez = jnp.exp(-jnp.abs(z))
    u = 1.0 + ez
    du = u - 1.0
    log1p_ez = jnp.where(du == 0.0, ez, jnp.log(u) * (ez / du))
    softplus = jnp.maximum(z, 0.0) + log1p_ez
    log_a = (-LRU_C) * r * softplus
    a = jnp.exp(log_a)
    th = jnp.tanh(log_a)
    b = jnp.sqrt(-2.0 * th / (1.0 - th)) * (ig * xc)

    ng = ts // SUBLANES
    a3 = a.reshape(ng, SUBLANES, dr)
    b3 = b.reshape(ng, SUBLANES, dr)
    sub = lax.broadcasted_iota(jnp.int32, a3.shape, 1)
    d = 1
    while d < SUBLANES:
        keep = sub >= d
        a_s = pltpu.roll(a3, d, axis=1)
        b_s = pltpu.roll(b3, d, axis=1)
        b3 = jnp.where(keep, a3 * b_s + b3, b3)
        a3 = jnp.where(keep, a3 * a_s, a3)
        d *= 2
    a_scr[...] = a3.reshape(ts, dr)
    b_scr[...] = b3.reshape(ts, dr)

    def group(g, hprev):
        r0 = pl.multiple_of(g * SUBLANES, SUBLANES)
        hg = a_scr[pl.ds(r0, SUBLANES), :] * hprev + b_scr[pl.ds(r0, SUBLANES), :]
        h_scr[pl.ds(r0, SUBLANES), :] = hg
        return jnp.broadcast_to(hg[SUBLANES - 1:SUBLANES, :], (SUBLANES, dr))

    hcar[...] = lax.fori_loop(0, ng, group, hcar[...], unroll=4)

    y = (h_scr[...] * gate).astype(BF16)
    x3 = x2_ref[0] + jnp.dot(y, wout_ref[...], preferred_element_type=F32)
    x3_ref[0] = x3

    h3 = _rms(x3, gn_ref[...], NORM_EPS)
    h_hi = h3.astype(BF16)
    h_lo = (h3 - h_hi.astype(F32)).astype(BF16)
    rT = rT_ref[...]
    r_hi = rT.astype(BF16)
    r_lo = (rT - r_hi.astype(F32)).astype(BF16)
    lg = lax.dot_general(r_hi, h_hi, NT_DIMS, preferred_element_type=F32)
    lg = lg + lax.dot_general(r_lo, h_hi, NT_DIMS, preferred_element_type=F32)
    lg = lg + lax.dot_general(r_hi, h_lo, NT_DIMS, preferred_element_type=F32)
    lg_ref[0] = lg


def _lru(h2, x2, win, cw, cb, gaw, gab, gxw, gxb, lam, wout, gn, rT, ts):
    B, S, D = x2.shape
    dr = wout.shape[0]
    E = rT.shape[0]
    kern = functools.partial(_lru_kernel, ts=ts, dr=dr)
    tile = lambda b, i: (b, i, 0)
    return pl.pallas_call(
        kern,
        grid=(B, S // ts),
        in_specs=[
            pl.BlockSpec((1, ts, D), tile),
            pl.BlockSpec((1, ts, D), tile),
            _const_spec(win.shape),
            _const_spec(cw.shape),
            _const_spec(cb.shape),
            _const_spec(gaw.shape),
            _const_spec(gab.shape),
            _const_spec(gxw.shape),
            _const_spec(gxb.shape),
            _const_spec(lam.shape),
            _const_spec(wout.shape),
            _const_spec(gn.shape),
            _const_spec(rT.shape),
        ],
        out_specs=[pl.BlockSpec((1, ts, D), tile),
                   pl.BlockSpec((1, E, ts), lambda b, i: (b, 0, i))],
        out_shape=[jax.ShapeDtypeStruct((B, S, D), F32),
                   jax.ShapeDtypeStruct((B, E, S), F32)],
        scratch_shapes=[
            pltpu.VMEM((SUBLANES + ts, dr), F32),
            pltpu.VMEM((ts, dr), F32),
            pltpu.VMEM((ts, dr), F32),
            pltpu.VMEM((ts, dr), F32),
            pltpu.VMEM((SUBLANES, dr), F32),
        ],
        compiler_params=pltpu.CompilerParams(
            dimension_semantics=("arbitrary", "arbitrary"),
            vmem_limit_bytes=_vmem_limit(48 * 1024 * 1024)),
        name="lru",
    )(h2, x2, win, cw, cb, gaw, gab, gxw, gxb, lam, wout, gn, rT)


def _route_kernel(lg_ref, idx_ref, gate_ref, cnt_ref, carry, *, tr):
    @pl.when(pl.program_id(0) == 0)
    def _():
        carry[...] = jnp.zeros(carry.shape, F32)

    lg = lg_ref[...]
    E = lg.shape[0]
    eidx = lax.broadcasted_iota(jnp.int32, lg.shape, 0).astype(F32)
    m1 = jnp.max(lg, axis=0, keepdims=True)
    i1 = jnp.min(jnp.where(lg == m1, eidx, float(E)), axis=0, keepdims=True)
    oh1 = eidx == i1
    lg2 = jnp.where(oh1, -jnp.inf, lg)
    m2 = jnp.max(lg2, axis=0, keepdims=True)
    i2 = jnp.min(jnp.where(lg2 == m2, eidx, float(E)), axis=0, keepdims=True)
    oh2 = eidx == i2
    ex = jnp.exp(m2 - m1)
    g1 = 1.0 / (1.0 + ex)
    g2 = ex / (1.0 + ex)

    both = jnp.where(oh1 | oh2, 1.0, 0.0).astype(BF16)
    ti = lax.broadcasted_iota(jnp.int32, (tr, tr), 0)
    tj = lax.broadcasted_iota(jnp.int32, (tr, tr), 1)
    before = jnp.where(ti < tj, 1.0, 0.0).astype(BF16)
    rank = jnp.dot(both, before, preferred_element_type=F32) + carry[...][:, 0:1]
    r1 = jnp.sum(jnp.where(oh1, rank, 0.0), axis=0, keepdims=True)
    r2 = jnp.sum(jnp.where(oh2, rank, 0.0), axis=0, keepdims=True)
    tot = carry[...] + jnp.sum(both.astype(F32), axis=1, keepdims=True)
    carry[...] = tot
    cnt_ref[...] = tot.astype(jnp.int32)

    orow = lax.broadcasted_iota(jnp.int32, (8, tr), 0)
    packed = jnp.where(orow == 0, i1, jnp.where(orow == 1, i2, jnp.where(orow == 2, r1, r2)))
    idx_ref[...] = packed.astype(jnp.int32)
    gate_ref[...] = jnp.where(orow == 0, g1, g2)


def _route(lgT, tr):
    E, T = lgT.shape
    kern = functools.partial(_route_kernel, tr=tr)
    return pl.pallas_call(
        kern,
        grid=(T // tr,),
        in_specs=[pl.BlockSpec((E, tr), lambda i: (0, i))],
        out_specs=[pl.BlockSpec((8, tr), lambda i: (0, i)),
                   pl.BlockSpec((8, tr), lambda i: (0, i)),
                   pl.BlockSpec((E, 128), lambda i: (0, 0))],
        out_shape=[jax.ShapeDtypeStruct((8, T), jnp.int32),
                   jax.ShapeDtypeStruct((8, T), F32),
                   jax.ShapeDtypeStruct((E, 128), jnp.int32)],
        scratch_shapes=[pltpu.VMEM((E, 128), F32)],
        compiler_params=pltpu.CompilerParams(dimension_semantics=("arbitrary",)),
        name="route",
    )(lgT)


def _row_copy(src, dst, s, d, sem):
    return pltpu.make_async_copy(src.at[pl.ds(s, 1)], dst.at[pl.ds(d, 1)], sem)


def _dispatch_kernel(pos_ref, x_ref, xs_in, xs_out, sem, *, td):
    del xs_in

    def issue(t, c):
        for kk in range(2):
            _row_copy(x_ref, xs_out, t, pos_ref[0, kk, t], sem).start(priority=kk)
        return c

    lax.fori_loop(0, td, issue, 0, unroll=8)

    def drain(t, c):
        for kk in range(2):
            _row_copy(x_ref, xs_out, 0, 0, sem).wait()
        return c

    lax.fori_loop(0, td, drain, 0, unroll=8)


def _dispatch(pos3, x, xs_zero, td):
    T, D = x.shape
    kern = functools.partial(_dispatch_kernel, td=td)
    return pl.pallas_call(
        kern,
        grid=(T // td,),
        in_specs=[
            pl.BlockSpec((1, 2, td), lambda i: (i, 0, 0), memory_space=pltpu.SMEM),
            pl.BlockSpec((td, D), lambda i: (i, 0)),
            pl.BlockSpec(memory_space=pl.ANY),
        ],
        out_specs=pl.BlockSpec(memory_space=pl.ANY),
        out_shape=jax.ShapeDtypeStruct(xs_zero.shape, xs_zero.dtype),
        scratch_shapes=[pltpu.SemaphoreType.DMA],
        input_output_aliases={2: 0},
        compiler_params=pltpu.CompilerParams(
            dimension_semantics=("arbitrary",), has_side_effects=True),
        name="dispatch",
    )(pos3, x, xs_zero)


def _expert_kernel(te_ref, nu_ref, xs_ref, gn_ref, wg_ref, wu_ref, wd_ref, ys_ref,
                   h_scr, acc_scr, *, nf):
    i = pl.program_id(0)
    j = pl.program_id(1)

    @pl.when(i < nu_ref[0])
    def _():
        @pl.when(j == 0)
        def _():
            h_scr[...] = _rms(xs_ref[...], gn_ref[...], NORM_EPS).astype(BF16)
            acc_scr[...] = jnp.zeros(acc_scr.shape, F32)

        h = h_scr[...]
        gt = jnp.dot(h, wg_ref[0], preferred_element_type=F32)
        up = jnp.dot(h, wu_ref[0], preferred_element_type=F32)
        a = (gt * jax.nn.sigmoid(gt) * up).astype(BF16)
        acc_scr[...] += jnp.dot(a, wd_ref[0], preferred_element_type=F32)

        @pl.when(j == nf - 1)
        def _():
            ys_ref[...] = acc_scr[...]

    @pl.when((i >= nu_ref[0]) & (j == nf - 1))
    def _():
        ys_ref[...] = jnp.zeros(ys_ref.shape, F32)


def _experts(tile_expert, n_used, xs, gn, wg, wu, wd, tm, nf):
    NP, D = xs.shape
    E, _, Fd = wg.shape
    fc = Fd // nf
    nt = NP // tm

    def eff(i, nu):
        return jnp.minimum(i, nu[0] - 1)

    def chunk(i, j, nu):
        ie = eff(i, nu)
        je = jnp.where(i < nu[0], j, nf - 1)
        return jnp.where(ie % 2 == 0, je, nf - 1 - je)

    kern = functools.partial(_expert_kernel, nf=nf)
    return pl.pallas_call(
        kern,
        grid_spec=pltpu.PrefetchScalarGridSpec(
            num_scalar_prefetch=2,
            grid=(nt, nf),
            in_specs=[
                pl.BlockSpec((tm, D), lambda i, j, te, nu: (eff(i, nu), 0)),
                pl.BlockSpec((1, D), lambda i, j, te, nu: (0, 0)),
                pl.BlockSpec((1, D, fc), lambda i, j, te, nu: (te[eff(i, nu)], 0, chunk(i, j, nu))),
                pl.BlockSpec((1, D, fc), lambda i, j, te, nu: (te[eff(i, nu)], 0, chunk(i, j, nu))),
                pl.BlockSpec((1, fc, D), lambda i, j, te, nu: (te[eff(i, nu)], chunk(i, j, nu), 0)),
            ],
            out_specs=pl.BlockSpec((tm, D), lambda i, j, te, nu: (i, 0)),
            scratch_shapes=[pltpu.VMEM((tm, D), BF16), pltpu.VMEM((tm, D), F32)],
        ),
        out_shape=jax.ShapeDtypeStruct((NP, D), F32),
        compiler_params=pltpu.CompilerParams(
            dimension_semantics=("arbitrary", "arbitrary"),
            vmem_limit_bytes=_vmem_limit(56 * 1024 * 1024)),
        name="experts",
    )(tile_expert, n_used, xs, gn, wg, wu, wd)


def _combine_kernel(pos_ref, gate_ref, x_ref, gn_ref, ys_hbm, out_ref, buf, sem, *, tc):
    def issue(t, c):
        for kk in range(2):
            pltpu.make_async_copy(ys_hbm.at[pl.ds(pos_ref[0, kk, t], 1)],
                                  buf.at[kk, pl.ds(t, 1)], sem).start(priority=kk)
        return c

    lax.fori_loop(0, tc, issue, 0, unroll=8)

    def drain(t, c):
        for kk in range(2):
            pltpu.make_async_copy(ys_hbm.at[pl.ds(0, 1)], buf.at[0, pl.ds(0, 1)], sem).wait()
        return c

    lax.fori_loop(0, tc, drain, 0, unroll=8)

    g = gate_ref[...]
    x4 = x_ref[...] + g[:, 0:1] * buf[0] + g[:, 1:2] * buf[1]
    out_ref[...] = _rms(x4, gn_ref[...], NORM_EPS)


def _combine(pos3, gates_col, x, gn, ys, tc):
    T, D = x.shape
    kern = functools.partial(_combine_kernel, tc=tc)
    return pl.pallas_call(
        kern,
        grid=(T // tc,),
        in_specs=[
            pl.BlockSpec((1, 2, tc), lambda i: (i, 0, 0), memory_space=pltpu.SMEM),
            pl.BlockSpec((tc, 2), lambda i: (i, 0)),
            pl.BlockSpec((tc, D), lambda i: (i, 0)),
            _const_spec((1, D)),
            pl.BlockSpec(memory_space=pl.ANY),
        ],
        out_specs=pl.BlockSpec((tc, D), lambda i: (i, 0)),
        out_shape=jax.ShapeDtypeStruct((T, D), F32),
        scratch_shapes=[pltpu.VMEM((2, tc, D), F32), pltpu.SemaphoreType.DMA],
        compiler_params=pltpu.CompilerParams(
            dimension_semantics=("arbitrary",),
            vmem_limit_bytes=_vmem_limit(40 * 1024 * 1024)),
        name="combine",
    )(pos3, gates_col, x, gn, ys)


def _tile(n, pref):
    t = min(n, pref)
    assert n % t == 0, (n, t)
    return t


def kernel(x, norm_mix, norm_ffn, norm_final, rel_bias, attn_w_qkv, attn_w_o, attn_lambda_q1, attn_lambda_k1, attn_lambda_q2, attn_lambda_k2, attn_subln, lru_w_in, lru_conv_w, lru_conv_b, lru_gate_a_w, lru_gate_a_b, lru_gate_x_w, lru_gate_x_b, lru_lambda, lru_w_out, ffn_w_gate, ffn_w_up, ffn_w_down, moe_router, moe_w_gate, moe_w_up, moe_w_down):
    B, S, D = x.shape
    T = B * S
    assert D == N_HEADS * V_DIM
    row = lambda v: v.reshape(1, -1).astype(F32)

    ta = _tile(S, ATTN_TILE)
    wqkv = attn_w_qkv[0]
    wqT = (wqkv[:, :D] * (SCALE * LOG2E)).T.astype(BF16)
    wk = wqkv[:, D:2 * D].astype(BF16)
    wvT = wqkv[:, 2 * D:].T.astype(BF16)
    qT, k, vT = _qkv(x, row(norm_mix[0]), wqT, wk, wvT, ta)
    lqk = jnp.stack([attn_lambda_q1[0], attn_lambda_k1[0],
                     attn_lambda_q2[0], attn_lambda_k2[0]]).astype(F32)
    bias = _bias_tiles(rel_bias, ta)
    o = _attention(lqk, qT, k, vT, bias, attn_subln[0].reshape(V_DIM, 1).astype(F32), ta)

    tm = _tile(T, TOKEN_TILE)
    x2, h2 = _ffn(o.reshape(T, D), x.reshape(T, D), attn_w_o[0].astype(BF16), row(norm_ffn[0]),
                  ffn_w_gate[0].astype(BF16), ffn_w_up[0].astype(BF16),
                  ffn_w_down[0].astype(BF16), row(norm_mix[1]), tm)

    ts = _tile(S, LRU_TILE)
    x3, lg = _lru(h2.reshape(B, S, D), x2.reshape(B, S, D), lru_w_in[0].astype(BF16),
                  lru_conv_w[0].astype(F32), row(lru_conv_b[0]),
                  lru_gate_a_w[0].astype(BF16), row(lru_gate_a_b[0]),
                  lru_gate_x_w[0].astype(BF16), row(lru_gate_x_b[0]),
                  row(lru_lambda[0]), lru_w_out[0].astype(BF16), row(norm_ffn[1]),
                  moe_router[0].T.astype(F32), ts)
    x3 = x3.reshape(T, D)
    E = moe_router.shape[-1]
    lgT = jnp.transpose(lg, (1, 0, 2)).reshape(E, T)

    idx, gates, counts = _route(lgT, _tile(T, ROUTE_TILE))
    rm = EXPERT_ROW_TILE
    cnt = counts[:, 0]
    padded = ((cnt + rm - 1) // rm) * rm
    ends = jnp.cumsum(padded)
    starts = ends - padded
    pos = jnp.stack([starts[idx[0]] + idx[2], starts[idx[1]] + idx[3]])
    n_tiles = (2 * T) // rm + E
    tile_expert = jnp.minimum(
        jnp.searchsorted(ends, jnp.arange(n_tiles, dtype=jnp.int32) * rm, side="right"),
        E - 1).astype(jnp.int32)
    n_used = (ends[-1] // rm).astype(jnp.int32).reshape(1)

    td = _tile(T, DISPATCH_TILE)
    pos3 = pos.reshape(2, T // td, td).transpose(1, 0, 2)
    xs = _dispatch(pos3, x3, jnp.zeros((n_tiles * rm, D), F32), td)
    ys = _experts(tile_expert, n_used, xs, row(norm_ffn[1]), moe_w_gate[0].astype(BF16),
                  moe_w_up[0].astype(BF16), moe_w_down[0].astype(BF16), rm, EXPERT_FF_CHUNKS)
    out = _combine(pos3, gates[:2].T, x3, row(norm_final), ys, td)
    return out.reshape(B, S, D)
```

```python
import functools
import math

import numpy as np
import jax
import jax.numpy as jnp
from jax import lax
from jax.experimental import pallas as pl
from jax.experimental.pallas import tpu as pltpu

F32 = jnp.float32
BF16 = jnp.bfloat16

N_HEADS = 8
HEAD_DIM = 64
V_DIM = 2 * HEAD_DIM
SCALE = HEAD_DIM ** -0.5
LOG2E = math.log2(math.e)
NEG_INF = -1e30
NUM_BUCKETS = 32
MAX_EXACT = NUM_BUCKETS // 2
MAX_DISTANCE = 128
N_LRU_BLOCKS = 10
LRU_BLOCK = 128
CONV_WIDTH = 4
LRU_C = 8.0
N_EXPERTS = 8
NORM_EPS = 1e-6
SUBLN_EPS = 1e-5
LAM_INIT_0 = 0.8 - 0.6 * math.exp(-0.3 * 0)

V7X_VMEM_BYTES = 64 * 1024 * 1024
SUBLANES = 8

ATTN_TILE = 512
TOKEN_TILE = 512
LRU_TILE = 256
ROUTE_TILE = 1024
EXPERT_ROW_TILE = 512
EXPERT_FF_CHUNKS = 2
DISPATCH_TILE = 512

NT_DIMS = (((1,), (1,)), ((), ()))


def _vmem_limit(nbytes):
    return int(min(nbytes, V7X_VMEM_BYTES - 6 * 1024 * 1024))


def _rms(x, g, eps):
    ms = jnp.mean(x * x, axis=-1, keepdims=True)
    return x * lax.rsqrt(ms + eps) * g


def _const_spec(shape):
    nd = len(shape)
    return pl.BlockSpec(shape, lambda *_: (0,) * nd, pipeline_mode=pl.Buffered(1))


def _qkv_kernel(x_ref, g_ref, wqT_ref, wk_ref, wvT_ref, qT_ref, k_ref, vT_ref):
    hn = _rms(x_ref[0], g_ref[...], NORM_EPS).astype(BF16)
    qT_ref[0, 0] = lax.dot_general(wqT_ref[...], hn, NT_DIMS,
                                   preferred_element_type=F32).astype(BF16)
    k_ref[0] = jnp.dot(hn, wk_ref[...], preferred_element_type=F32).astype(BF16)
    vT_ref[0, 0] = lax.dot_general(wvT_ref[...], hn, NT_DIMS,
                                   preferred_element_type=F32).astype(BF16)


def _qkv(x, g, wqT, wk, wvT, tm):
    B, S, D = x.shape
    n = S // tm
    return pl.pallas_call(
        _qkv_kernel,
        grid=(B, n),
        in_specs=[
            pl.BlockSpec((1, tm, D), lambda b, i: (b, i, 0)),
            _const_spec((1, D)),
            _const_spec((D, D)),
            _const_spec((D, D)),
            _const_spec((D, D)),
        ],
        out_specs=[
            pl.BlockSpec((1, 1, D, tm), lambda b, i: (b, i, 0, 0)),
            pl.BlockSpec((1, tm, D), lambda b, i: (b, i, 0)),
            pl.BlockSpec((1, 1, D, tm), lambda b, i: (b, i, 0, 0)),
        ],
        out_shape=[
            jax.ShapeDtypeStruct((B, n, D, tm), BF16),
            jax.ShapeDtypeStruct((B, S, D), BF16),
            jax.ShapeDtypeStruct((B, n, D, tm), BF16),
        ],
        compiler_params=pltpu.CompilerParams(
            dimension_semantics=("arbitrary", "arbitrary"),
            vmem_limit_bytes=_vmem_limit(40 * 1024 * 1024)),
        name="qkv",
    )(x, g, wqT, wk, wvT)


def _t5_bucket_np(n):
    nf = np.maximum(n, 1).astype(np.float32)
    large = MAX_EXACT + (np.log(nf / np.float32(MAX_EXACT))
                         / np.float32(math.log(MAX_DISTANCE / MAX_EXACT))
                         * np.float32(NUM_BUCKETS - MAX_EXACT)).astype(np.int32)
    large = np.minimum(large, NUM_BUCKETS - 1)
    return np.where(n < MAX_EXACT, n, large)


def _bias_kernel(w_ref, out_ref, *, t):
    full = jnp.broadcast_to(w_ref[0], (t, 4 * t))
    c = pltpu.roll(full, 0, axis=1, stride=1, stride_axis=0)
    out_ref[0, 0] = c[:, :t]
    out_ref[0, 1] = c[:, t:2 * t]


def _bias_tiles(rel_bias, t):
    assert t >= MAX_DISTANCE and _t5_bucket_np(np.array([t + 1]))[0] == NUM_BUCKETS - 1
    H = rel_bias.shape[1]
    rb = rel_bias.astype(F32) - rel_bias[NUM_BUCKETS - 1].astype(F32)[None, :]
    near = jnp.take(rb, jnp.asarray(_t5_bucket_np(np.arange(2 * t))), axis=0).T
    w = jnp.concatenate([near, jnp.full((H, 2 * t), NEG_INF, F32)], axis=1) * LOG2E
    return pl.pallas_call(
        functools.partial(_bias_kernel, t=t),
        grid=(H,),
        in_specs=[pl.BlockSpec((1, 1, 4 * t), lambda h: (h, 0, 0))],
        out_specs=pl.BlockSpec((1, 2, t, t), lambda h: (h, 0, 0, 0)),
        out_shape=jax.ShapeDtypeStruct((H, 2, t, t), F32),
        compiler_params=pltpu.CompilerParams(dimension_semantics=("arbitrary",)),
        name="bias",
    )(w.reshape(H, 1, 4 * t))


def _attn_kernel(lqk_ref, qT_ref, k_ref, vT_ref, bias_ref, g_ref, o_ref,
                 q_scr, sa_scr, sb_scr, sc_scr, mba_scr, mbb_scr, mbc_scr, m_scr, l_scr, acc_scr,
                 *, t, n):
    row = lax.broadcasted_iota(jnp.int32, (V_DIM, t), 0)
    for i in range(n):
        qT = qT_ref[0, i]
        zero = jnp.zeros_like(qT)
        q_scr[i, 0] = jnp.where(row < HEAD_DIM, qT, zero)
        q_scr[i, 1] = jnp.where(row >= HEAD_DIM, qT, zero)

    lqk = lqk_ref[...]
    e1 = jnp.exp(jnp.sum(lqk[0:1] * lqk[1:2], axis=-1, keepdims=True))
    e2 = jnp.exp(jnp.sum(lqk[2:3] * lqk[3:4], axis=-1, keepdims=True))
    lam = e1 - e2 + LAM_INIT_0

    def reset():
        m_scr[...] = jnp.full(m_scr.shape, NEG_INF, F32)
        l_scr[...] = jnp.zeros(l_scr.shape, F32)
        acc_scr[...] = jnp.zeros(acc_scr.shape, F32)

    def scores(qi, j, bias_idx, s_scr, mb_scr):
        kc = k_ref[0, pl.ds(pl.multiple_of(j * t, t), t), :]
        for c in range(2):
            s = jnp.dot(kc, q_scr[qi, c], preferred_element_type=F32)
            if bias_idx is not None:
                s = s + bias_ref[0, bias_idx]
            s_scr[c] = s
            mb_scr[c] = jnp.max(s, axis=0, keepdims=True)

    def consume(j, s_scr, mb_scr):
        vt = vT_ref[0, j]
        for c in range(2):
            m_old = m_scr[c]
            m_new = jnp.maximum(m_old, mb_scr[c])
            alpha = jnp.exp2(m_old - m_new)
            p = jnp.exp2(s_scr[c] - m_new)
            l_scr[c] = alpha * l_scr[c] + jnp.sum(p, axis=0, keepdims=True)
            acc_scr[c] = alpha * acc_scr[c] + jnp.dot(
                vt, p.astype(BF16), preferred_element_type=F32)
            m_scr[c] = m_new

    def finalize(qi):
        o = acc_scr[0] / l_scr[0] - lam * (acc_scr[1] / l_scr[1])
        ms = jnp.mean(o * o, axis=0, keepdims=True)
        on = o * lax.rsqrt(ms + SUBLN_EPS) * g_ref[...] * (1.0 - LAM_INIT_0)
        o_ref[0, pl.ds(pl.multiple_of(qi * t, t), t), :] = on.T.astype(BF16)

    a_bufs = (sa_scr, mba_scr)
    b_bufs = (sb_scr, mbb_scr)
    c_bufs = (sc_scr, mbc_scr)

    reset()
    scores(0, 0, 0, *c_bufs)
    consume(0, *c_bufs)
    finalize(0)
    if n == 1:
        return
    scores(1, 1, 0, *c_bufs)

    def tile(qi, carry):
        reset()
        scores(qi, qi - 1, 1, *a_bufs)
        consume(qi, *c_bufs)
        n_far = qi - 1

        def pair(pp, c2):
            j0 = qi - 2 - 2 * pp
            scores(qi, j0, None, *b_bufs)
            consume(j0 + 1, *a_bufs)
            scores(qi, j0 - 1, None, *a_bufs)
            consume(j0, *b_bufs)
            return c2

        lax.fori_loop(0, n_far // 2, pair, 0)
        nxt = jnp.minimum(qi + 1, n - 1)

        @pl.when(n_far % 2 == 1)
        def _():
            scores(qi, 0, None, *b_bufs)
            consume(1, *a_bufs)
            scores(nxt, nxt, 0, *c_bufs)
            consume(0, *b_bufs)

        @pl.when(n_far % 2 == 0)
        def _():
            scores(nxt, nxt, 0, *c_bufs)
            consume(0, *a_bufs)

        finalize(qi)
        return carry

    lax.fori_loop(1, n, tile, 0)


def _attention(lqk, qT, k, vT, bias, g_col, t):
    B, n, D, _ = qT.shape
    S = n * t
    H = D // V_DIM
    kern = functools.partial(_attn_kernel, t=t, n=n)
    return pl.pallas_call(
        kern,
        grid=(B, H),
        in_specs=[
            _const_spec(lqk.shape),
            pl.BlockSpec((1, n, V_DIM, t), lambda b, h: (b, 0, h, 0)),
            pl.BlockSpec((1, S, V_DIM), lambda b, h: (b, 0, h)),
            pl.BlockSpec((1, n, V_DIM, t), lambda b, h: (b, 0, h, 0)),
            pl.BlockSpec((1, 2, t, t), lambda b, h: (h, 0, 0, 0)),
            _const_spec((V_DIM, 1)),
        ],
        out_specs=pl.BlockSpec((1, S, V_DIM), lambda b, h: (b, 0, h)),
        out_shape=jax.ShapeDtypeStruct((B, S, D), BF16),
        scratch_shapes=[
            pltpu.VMEM((n, 2, V_DIM, t), BF16),
            pltpu.VMEM((2, t, t), F32),
            pltpu.VMEM((2, t, t), F32),
            pltpu.VMEM((2, t, t), F32),
            pltpu.VMEM((2, 1, t), F32),
            pltpu.VMEM((2, 1, t), F32),
            pltpu.VMEM((2, 1, t), F32),
            pltpu.VMEM((2, 1, t), F32),
            pltpu.VMEM((2, 1, t), F32),
            pltpu.VMEM((2, V_DIM, t), F32),
        ],
        compiler_params=pltpu.CompilerParams(
            dimension_semantics=("arbitrary", "arbitrary"),
            vmem_limit_bytes=_vmem_limit(52 * 1024 * 1024)),
        name="attn",
    )(lqk, qT, k, vT, bias, g_col)


def _ffn_kernel(o_ref, x_ref, wo_ref, g1_ref, wg_ref, wu_ref, wd_ref, g2_ref,
                x2_ref, h2_ref, *, n_chunks):
    x1 = x_ref[...] + jnp.dot(o_ref[...], wo_ref[...], preferred_element_type=F32)
    h1 = _rms(x1, g1_ref[...], NORM_EPS).astype(BF16)
    fc = wg_ref.shape[1] // n_chunks
    f = jnp.zeros_like(x1)
    for c in range(n_chunks):
        sl = slice(c * fc, (c + 1) * fc)
        gt = jnp.dot(h1, wg_ref[:, sl], preferred_element_type=F32)
        up = jnp.dot(h1, wu_ref[:, sl], preferred_element_type=F32)
        a = (gt * jax.nn.sigmoid(gt) * up).astype(BF16)
        f = f + jnp.dot(a, wd_ref[sl, :], preferred_element_type=F32)
    x2 = x1 + f
    x2_ref[...] = x2
    h2_ref[...] = _rms(x2, g2_ref[...], NORM_EPS).astype(BF16)


def _ffn(o, x, wo, g1, wg, wu, wd, g2, tm):
    T, D = x.shape
    Fd = wg.shape[1]
    n_chunks = 2 if (Fd // 2) % 128 == 0 else 1
    kern = functools.partial(_ffn_kernel, n_chunks=n_chunks)
    tile = lambda i: (i, 0)
    return pl.pallas_call(
        kern,
        grid=(T // tm,),
        in_specs=[
            pl.BlockSpec((tm, D), tile),
            pl.BlockSpec((tm, D), tile),
            _const_spec((D, D)),
            _const_spec((1, D)),
            _const_spec((D, Fd)),
            _const_spec((D, Fd)),
            _const_spec((Fd, D)),
            _const_spec((1, D)),
        ],
        out_specs=[pl.BlockSpec((tm, D), tile), pl.BlockSpec((tm, D), tile)],
        out_shape=[jax.ShapeDtypeStruct((T, D), F32), jax.ShapeDtypeStruct((T, D), BF16)],
        compiler_params=pltpu.CompilerParams(
            dimension_semantics=("arbitrary",),
            vmem_limit_bytes=_vmem_limit(56 * 1024 * 1024)),
        name="ffn",
    )(o, x, wo, g1, wg, wu, wd, g2)


def _gelu_tanh(x):
    return 0.5 * x * (1.0 + jnp.tanh(math.sqrt(2.0 / math.pi) * (x + 0.044715 * (x * x * x))))


def _lru_kernel(h2_ref, x2_ref, win_ref, cw_ref, cb_ref, gaw_ref, gab_ref, gxw_ref, gxb_ref,
                lam_ref, wout_ref, gn_ref, rT_ref, x3_ref, lg_ref,
                xbuf, a_scr, b_scr, h_scr, hcar, *, ts, dr):
    si = pl.program_id(1)
    halo = SUBLANES

    @pl.when(si == 0)
    def _():
        xbuf[0:halo, :] = jnp.zeros((halo, dr), F32)
        hcar[...] = jnp.zeros(hcar.shape, F32)

    gx = jnp.dot(h2_ref[0], win_ref[...], preferred_element_type=F32)
    gate = _gelu_tanh(gx[:, :dr])
    xb = gx[:, dr:]
    xbuf[halo:halo + ts, :] = xb

    cw = cw_ref[...]
    xc = cb_ref[...] + cw[CONV_WIDTH - 1:CONV_WIDTH] * xb
    for j in range(CONV_WIDTH - 1):
        back = CONV_WIDTH - 1 - j
        xc = xc + cw[j:j + 1] * xbuf[halo - back:halo - back + ts, :]
    xbuf[0:halo, :] = xb[ts - halo:ts, :]

    xcb = xc.astype(BF16)
    r_parts, i_parts = [], []
    for n in range(dr // LRU_BLOCK):
        blk = xcb[:, n * LRU_BLOCK:(n + 1) * LRU_BLOCK]
        r_parts.append(jnp.dot(blk, gaw_ref[n], preferred_element_type=F32))
        i_parts.append(jnp.dot(blk, gxw_ref[n], preferred_element_type=F32))
    r = jax.nn.sigmoid(jnp.concatenate(r_parts, axis=1) + gab_ref[...])
    ig = jax.nn.sigmoid(jnp.concatenate(i_parts, axis=1) + gxb_ref[...])
    z = -lam_ref[...]
    ez = jnp.exp(-jnp.abs(z))
    u = 1.0 + ez
    du = u - 1.0
    log1p_ez = jnp.where(du == 0.0, ez, jnp.log(u) * (ez / du))
    softplus = jnp.maximum(z, 0.0) + log1p_ez
    log_a = (-LRU_C) * r * softplus
    a = jnp.exp(log_a)
    th = jnp.tanh(log_a)
    b = jnp.sqrt(-2.0 * th / (1.0 - th)) * (ig * xc)

    ng = ts // SUBLANES
    a3 = a.reshape(ng, SUBLANES, dr)
    b3 = b.reshape(ng, SUBLANES, dr)
    sub = lax.broadcasted_iota(jnp.int32, a3.shape, 1)
    d = 1
    while d < SUBLANES:
        keep = sub >= d
        a_s = pltpu.roll(a3, d, axis=1)
        b_s = pltpu.roll(b3, d, axis=1)
        b3 = jnp.where(keep, a3 * b_s + b3, b3)
        a3 = jnp.where(keep, a3 * a_s, a3)
        d *= 2
    a_scr[...] = a3.reshape(ts, dr)
    b_scr[...] = b3.reshape(ts, dr)

    def group(g, hprev):
        r0 = pl.multiple_of(g * SUBLANES, SUBLANES)
        hg = a_scr[pl.ds(r0, SUBLANES), :] * hprev + b_scr[pl.ds(r0, SUBLANES), :]
        h_scr[pl.ds(r0, SUBLANES), :] = hg
        return jnp.broadcast_to(hg[SUBLANES - 1:SUBLANES, :], (SUBLANES, dr))

    hcar[...] = lax.fori_loop(0, ng, group, hcar[...], unroll=4)

    y = (h_scr[...] * gate).astype(BF16)
    x3 = x2_ref[0] + jnp.dot(y, wout_ref[...], preferred_element_type=F32)
    x3_ref[0] = x3

    h3 = _rms(x3, gn_ref[...], NORM_EPS)
    h_hi = h3.astype(BF16)
    h_lo = (h3 - h_hi.astype(F32)).astype(BF16)
    rT = rT_ref[...]
    r_hi = rT.astype(BF16)
    r_lo = (rT - r_hi.astype(F32)).astype(BF16)
    lg = lax.dot_general(r_hi, h_hi, NT_DIMS, preferred_element_type=F32)
    lg = lg + lax.dot_general(r_lo, h_hi, NT_DIMS, preferred_element_type=F32)
    lg = lg + lax.dot_general(r_hi, h_lo, NT_DIMS, preferred_element_type=F32)
    lg_ref[0] = lg


def _lru(h2, x2, win, cw, cb, gaw, gab, gxw, gxb, lam, wout, gn, rT, ts):
    B, S, D = x2.shape
    dr = wout.shape[0]
    E = rT.shape[0]
    kern = functools.partial(_lru_kernel, ts=ts, dr=dr)
    tile = lambda b, i: (b, i, 0)
    return pl.pallas_call(
        kern,
        grid=(B, S // ts),
        in_specs=[
            pl.BlockSpec((1, ts, D), tile),
            pl.BlockSpec((1, ts, D), tile),
            _const_spec(win.shape),
            _const_spec(cw.shape),
            _const_spec(cb.shape),
            _const_spec(gaw.shape),
            _const_spec(gab.shape),
            _const_spec(gxw.shape),
            _const_spec(gxb.shape),
            _const_spec(lam.shape),
            _const_spec(wout.shape),
            _const_spec(gn.shape),
            _const_spec(rT.shape),
        ],
        out_specs=[pl.BlockSpec((1, ts, D), tile),
                   pl.BlockSpec((1, E, ts), lambda b, i: (b, 0, i))],
        out_shape=[jax.ShapeDtypeStruct((B, S, D), F32),
                   jax.ShapeDtypeStruct((B, E, S), F32)],
        scratch_shapes=[
            pltpu.VMEM((SUBLANES + ts, dr), F32),
            pltpu.VMEM((ts, dr), F32),
            pltpu.VMEM((ts, dr), F32),
            pltpu.VMEM((ts, dr), F32),
            pltpu.VMEM((SUBLANES, dr), F32),
        ],
        compiler_params=pltpu.CompilerParams(
            dimension_semantics=("arbitrary", "arbitrary"),
            vmem_limit_bytes=_vmem_limit(48 * 1024 * 1024)),
        name="lru",
    )(h2, x2, win, cw, cb, gaw, gab, gxw, gxb, lam, wout, gn, rT)


def _route_kernel(lg_ref, idx_ref, gate_ref, cnt_ref, carry, *, tr):
    @pl.when(pl.program_id(0) == 0)
    def _():
        carry[...] = jnp.zeros(carry.shape, F32)

    lg = lg_ref[...]
    E = lg.shape[0]
    eidx = lax.broadcasted_iota(jnp.int32, lg.shape, 0).astype(F32)
    m1 = jnp.max(lg, axis=0, keepdims=True)
    i1 = jnp.min(jnp.where(lg == m1, eidx, float(E)), axis=0, keepdims=True)
    oh1 = eidx == i1
    lg2 = jnp.where(oh1, -jnp.inf, lg)
    m2 = jnp.max(lg2, axis=0, keepdims=True)
    i2 = jnp.min(jnp.where(lg2 == m2, eidx, float(E)), axis=0, keepdims=True)
    oh2 = eidx == i2
    ex = jnp.exp(m2 - m1)
    g1 = 1.0 / (1.0 + ex)
    g2 = ex / (1.0 + ex)

    both = jnp.where(oh1 | oh2, 1.0, 0.0).astype(BF16)
    ti = lax.broadcasted_iota(jnp.int32, (tr, tr), 0)
    tj = lax.broadcasted_iota(jnp.int32, (tr, tr), 1)
    before = jnp.where(ti < tj, 1.0, 0.0).astype(BF16)
    rank = jnp.dot(both, before, preferred_element_type=F32) + carry[...][:, 0:1]
    r1 = jnp.sum(jnp.where(oh1, rank, 0.0), axis=0, keepdims=True)
    r2 = jnp.sum(jnp.where(oh2, rank, 0.0), axis=0, keepdims=True)
    tot = carry[...] + jnp.sum(both.astype(F32), axis=1, keepdims=True)
    carry[...] = tot
    cnt_ref[...] = tot.astype(jnp.int32)

    orow = lax.broadcasted_iota(jnp.int32, (8, tr), 0)
    packed = jnp.where(orow == 0, i1, jnp.where(orow == 1, i2, jnp.where(orow == 2, r1, r2)))
    idx_ref[...] = packed.astype(jnp.int32)
    gate_ref[...] = jnp.where(orow == 0, g1, g2)


def _route(lgT, tr):
    E, T = lgT.shape
    kern = functools.partial(_route_kernel, tr=tr)
    return pl.pallas_call(
        kern,
        grid=(T // tr,),
        in_specs=[pl.BlockSpec((E, tr), lambda i: (0, i))],
        out_specs=[pl.BlockSpec((8, tr), lambda i: (0, i)),
                   pl.BlockSpec((8, tr), lambda i: (0, i)),
                   pl.BlockSpec((E, 128), lambda i: (0, 0))],
        out_shape=[jax.ShapeDtypeStruct((8, T), jnp.int32),
                   jax.ShapeDtypeStruct((8, T), F32),
                   jax.ShapeDtypeStruct((E, 128), jnp.int32)],
        scratch_shapes=[pltpu.VMEM((E, 128), F32)],
        compiler_params=pltpu.CompilerParams(dimension_semantics=("arbitrary",)),
        name="route",
    )(lgT)


def _dispatch_kernel(zt_ref, pos_ref, x_ref, xs_out, zero_scr, sem, zsem, *, td, rm):

    @pl.when(pl.program_id(0) == 0)
    def _():
        zero_scr[...] = jnp.zeros(zero_scr.shape, F32)
        for i in range(zt_ref.shape[0]):
            dst = xs_out.at[pl.ds(pl.multiple_of(zt_ref[i] * rm, rm), rm)]
            cp = pltpu.make_async_copy(zero_scr, dst, zsem)
            cp.start()
            cp.wait()

    def row_copy(g, u, k, dst_row):
        return pltpu.make_async_copy(x_ref.at[g, pl.ds(u, 1)], xs_out.at[pl.ds(dst_row, 1)], sem)

    def issue(g, c):
        for u in range(SUBLANES):
            for k in range(2):
                row_copy(g, u, k, pos_ref[g, 2 * u + k]).start(priority=k)
        return c

    lax.fori_loop(0, td // SUBLANES, issue, 0)

    def drain(g, c):
        for u in range(SUBLANES):
            for k in range(2):
                row_copy(0, 0, k, 0).wait()
        return c

    lax.fori_loop(0, td // SUBLANES, drain, 0)


def _dispatch(zero_tiles, pos_g, x, n_rows, td, rm):
    T, D = x.shape
    kern = functools.partial(_dispatch_kernel, td=td, rm=rm)
    gd = td // SUBLANES
    return pl.pallas_call(
        kern,
        grid_spec=pltpu.PrefetchScalarGridSpec(
            num_scalar_prefetch=1,
            grid=(T // td,),
            in_specs=[
                pl.BlockSpec((gd, 2 * SUBLANES), lambda i, zt: (i, 0), memory_space=pltpu.SMEM),
                pl.BlockSpec((gd, SUBLANES, D), lambda i, zt: (i, 0, 0)),
            ],
            out_specs=pl.BlockSpec(memory_space=pl.ANY),
            scratch_shapes=[pltpu.VMEM((rm, D), F32), pltpu.SemaphoreType.DMA,
                            pltpu.SemaphoreType.DMA],
        ),
        out_shape=jax.ShapeDtypeStruct((n_rows, D), F32),
        compiler_params=pltpu.CompilerParams(
            dimension_semantics=("arbitrary",), has_side_effects=True),
        name="dispatch",
    )(zero_tiles, pos_g, x.reshape(T // SUBLANES, SUBLANES, D))


def _expert_kernel(te_ref, nu_ref, xs_ref, gn_ref, wg_ref, wu_ref, wd_ref, ys_ref,
                   h_scr, acc_scr, *, nf):
    i = pl.program_id(0)
    j = pl.program_id(1)

    @pl.when(i < nu_ref[0])
    def _():
        @pl.when(j == 0)
        def _():
            h_scr[...] = _rms(xs_ref[...], gn_ref[...], NORM_EPS).astype(BF16)
            acc_scr[...] = jnp.zeros(acc_scr.shape, F32)

        h = h_scr[...]
        gt = jnp.dot(h, wg_ref[0], preferred_element_type=F32)
        up = jnp.dot(h, wu_ref[0], preferred_element_type=F32)
        a = (gt * jax.nn.sigmoid(gt) * up).astype(BF16)
        acc_scr[...] += jnp.dot(a, wd_ref[0], preferred_element_type=F32)

        @pl.when(j == nf - 1)
        def _():
            ys_ref[...] = acc_scr[...]

    @pl.when((i >= nu_ref[0]) & (j == nf - 1))
    def _():
        ys_ref[...] = jnp.zeros(ys_ref.shape, F32)


def _experts(tile_expert, n_used, xs, gn, wg, wu, wd, tm, nf):
    NP, D = xs.shape
    E, _, Fd = wg.shape
    fc = Fd // nf
    nt = NP // tm

    def eff(i, nu):
        return jnp.minimum(i, nu[0] - 1)

    def chunk(i, j, nu):
        ie = eff(i, nu)
        je = jnp.where(i < nu[0], j, nf - 1)
        return jnp.where(ie % 2 == 0, je, nf - 1 - je)

    kern = functools.partial(_expert_kernel, nf=nf)
    return pl.pallas_call(
        kern,
        grid_spec=pltpu.PrefetchScalarGridSpec(
            num_scalar_prefetch=2,
            grid=(nt, nf),
            in_specs=[
                pl.BlockSpec((tm, D), lambda i, j, te, nu: (eff(i, nu), 0)),
                pl.BlockSpec((1, D), lambda i, j, te, nu: (0, 0)),
                pl.BlockSpec((1, D, fc), lambda i, j, te, nu: (te[eff(i, nu)], 0, chunk(i, j, nu))),
                pl.BlockSpec((1, D, fc), lambda i, j, te, nu: (te[eff(i, nu)], 0, chunk(i, j, nu))),
                pl.BlockSpec((1, fc, D), lambda i, j, te, nu: (te[eff(i, nu)], chunk(i, j, nu), 0)),
            ],
            out_specs=pl.BlockSpec((tm, D), lambda i, j, te, nu: (i, 0)),
            scratch_shapes=[pltpu.VMEM((tm, D), BF16), pltpu.VMEM((tm, D), F32)],
        ),
        out_shape=jax.ShapeDtypeStruct((NP, D), F32),
        compiler_params=pltpu.CompilerParams(
            dimension_semantics=("arbitrary", "arbitrary"),
            vmem_limit_bytes=_vmem_limit(56 * 1024 * 1024)),
        name="experts",
    )(tile_expert, n_used, xs, gn, wg, wu, wd)


def _combine_kernel(pos_ref, gate_ref, x_ref, gn_ref, ys_hbm, out_ref, buf, sem, *, tc):
    def row_copy(g, u, k, src_row):
        return pltpu.make_async_copy(ys_hbm.at[pl.ds(src_row, 1)], buf.at[k, g, pl.ds(u, 1)], sem)

    def issue(g, c):
        for u in range(SUBLANES):
            for k in range(2):
                row_copy(g, u, k, pos_ref[g, 2 * u + k]).start(priority=k)
        return c

    lax.fori_loop(0, tc // SUBLANES, issue, 0)

    def drain(g, c):
        for u in range(SUBLANES):
            for k in range(2):
                row_copy(0, 0, k, 0).wait()
        return c

    lax.fori_loop(0, tc // SUBLANES, drain, 0)

    g = gate_ref[...]
    D = x_ref.shape[-1]
    x4 = (x_ref[...] + g[:, 0:1] * buf[0].reshape(tc, D) + g[:, 1:2] * buf[1].reshape(tc, D))
    out_ref[...] = _rms(x4, gn_ref[...], NORM_EPS)


def _combine(pos_g, gates_col, x, gn, ys, tc):
    T, D = x.shape
    kern = functools.partial(_combine_kernel, tc=tc)
    gc = tc // SUBLANES
    return pl.pallas_call(
        kern,
        grid=(T // tc,),
        in_specs=[
            pl.BlockSpec((gc, 2 * SUBLANES), lambda i: (i, 0), memory_space=pltpu.SMEM),
            pl.BlockSpec((tc, 2), lambda i: (i, 0)),
            pl.BlockSpec((tc, D), lambda i: (i, 0)),
            _const_spec((1, D)),
            pl.BlockSpec(memory_space=pl.ANY),
        ],
        out_specs=pl.BlockSpec((tc, D), lambda i: (i, 0)),
        out_shape=jax.ShapeDtypeStruct((T, D), F32),
        scratch_shapes=[pltpu.VMEM((2, gc, SUBLANES, D), F32), pltpu.SemaphoreType.DMA],
        compiler_params=pltpu.CompilerParams(
            dimension_semantics=("arbitrary",),
            vmem_limit_bytes=_vmem_limit(40 * 1024 * 1024)),
        name="combine",
    )(pos_g, gates_col, x, gn, ys)


def _tile(n, pref):
    t = min(n, pref)
    assert n % t == 0, (n, t)
    return t


def kernel(x, norm_mix, norm_ffn, norm_final, rel_bias, attn_w_qkv, attn_w_o, attn_lambda_q1, attn_lambda_k1, attn_lambda_q2, attn_lambda_k2, attn_subln, lru_w_in, lru_conv_w, lru_conv_b, lru_gate_a_w, lru_gate_a_b, lru_gate_x_w, lru_gate_x_b, lru_lambda, lru_w_out, ffn_w_gate, ffn_w_up, ffn_w_down, moe_router, moe_w_gate, moe_w_up, moe_w_down):
    B, S, D = x.shape
    T = B * S
    assert D == N_HEADS * V_DIM
    row = lambda v: v.reshape(1, -1).astype(F32)

    ta = _tile(S, ATTN_TILE)
    wqkv = attn_w_qkv[0]
    wqT = (wqkv[:, :D] * (SCALE * LOG2E)).T.astype(BF16)
    wk = wqkv[:, D:2 * D].astype(BF16)
    wvT = wqkv[:, 2 * D:].T.astype(BF16)
    qT, k, vT = _qkv(x, row(norm_mix[0]), wqT, wk, wvT, ta)
    lqk = jnp.stack([attn_lambda_q1[0], attn_lambda_k1[0],
                     attn_lambda_q2[0], attn_lambda_k2[0]]).astype(F32)
    bias = _bias_tiles(rel_bias, ta)
    o = _attention(lqk, qT, k, vT, bias, attn_subln[0].reshape(V_DIM, 1).astype(F32), ta)

    tm = _tile(T, TOKEN_TILE)
    x2, h2 = _ffn(o.reshape(T, D), x.reshape(T, D), attn_w_o[0].astype(BF16), row(norm_ffn[0]),
                  ffn_w_gate[0].astype(BF16), ffn_w_up[0].astype(BF16),
                  ffn_w_down[0].astype(BF16), row(norm_mix[1]), tm)

    ts = _tile(S, LRU_TILE)
    x3, lg = _lru(h2.reshape(B, S, D), x2.reshape(B, S, D), lru_w_in[0].astype(BF16),
                  lru_conv_w[0].astype(F32), row(lru_conv_b[0]),
                  lru_gate_a_w[0].astype(BF16), row(lru_gate_a_b[0]),
                  lru_gate_x_w[0].astype(BF16), row(lru_gate_x_b[0]),
                  row(lru_lambda[0]), lru_w_out[0].astype(BF16), row(norm_ffn[1]),
                  moe_router[0].T.astype(F32), ts)
    x3 = x3.reshape(T, D)
    E = moe_router.shape[-1]
    lgT = jnp.transpose(lg, (1, 0, 2)).reshape(E, T)

    idx, gates, counts = _route(lgT, _tile(T, ROUTE_TILE))
    rm = EXPERT_ROW_TILE
    cnt = counts[:, 0]
    padded = ((cnt + rm - 1) // rm) * rm
    ends = jnp.cumsum(padded)
    starts = ends - padded
    pos = jnp.stack([starts[idx[0]] + idx[2], starts[idx[1]] + idx[3]])
    n_tiles = (2 * T) // rm + E
    tile_expert = jnp.minimum(
        jnp.searchsorted(ends, jnp.arange(n_tiles, dtype=jnp.int32) * rm, side="right"),
        E - 1).astype(jnp.int32)
    n_used = (ends[-1] // rm).astype(jnp.int32).reshape(1)

    last_tile = jnp.maximum(ends // rm - 1, 0)
    tail_tile = jnp.minimum(n_used[0] + jnp.arange(E, dtype=jnp.int32), n_tiles - 1)
    zero_tiles = jnp.concatenate([last_tile, tail_tile]).astype(jnp.int32)

    td = _tile(T, DISPATCH_TILE)
    pos_g = pos.T.reshape(T // SUBLANES, 2 * SUBLANES)
    xs = _dispatch(zero_tiles, pos_g, x3, n_tiles * rm, td, rm)
    ys = _experts(tile_expert, n_used, xs, row(norm_ffn[1]), moe_w_gate[0].astype(BF16),
                  moe_w_up[0].astype(BF16), moe_w_down[0].astype(BF16), rm, EXPERT_FF_CHUNKS)
    out = _combine(pos_g, gates[:2].T, x3, row(norm_final), ys, td)
    return out.reshape(B, S, D)
```

```python
import functools
import math

import numpy as np
import jax
import jax.numpy as jnp
from jax import lax
from jax.experimental import pallas as pl
from jax.experimental.pallas import tpu as pltpu

F32 = jnp.float32
BF16 = jnp.bfloat16

N_HEADS = 8
HEAD_DIM = 64
V_DIM = 2 * HEAD_DIM
SCALE = HEAD_DIM ** -0.5
LOG2E = math.log2(math.e)
NEG_INF = -1e30
NUM_BUCKETS = 32
MAX_EXACT = NUM_BUCKETS // 2
MAX_DISTANCE = 128
N_LRU_BLOCKS = 10
LRU_BLOCK = 128
CONV_WIDTH = 4
LRU_C = 8.0
N_EXPERTS = 8
NORM_EPS = 1e-6
SUBLN_EPS = 1e-5
LAM_INIT_0 = 0.8 - 0.6 * math.exp(-0.3 * 0)

V7X_VMEM_BYTES = 64 * 1024 * 1024
SUBLANES = 8
BF16_ROWS = 16

ATTN_TILE = 512
TOKEN_TILE = 512
LRU_TILE = 256
ROUTE_TILE = 1024
EXPERT_ROW_TILE = 512
EXPERT_FF_CHUNKS = 2
FF_SLAB = 256
DISPATCH_TILE = 512

NT_DIMS = (((1,), (1,)), ((), ()))


def _vmem_limit(nbytes):
    return int(min(nbytes, V7X_VMEM_BYTES - 6 * 1024 * 1024))


def _rms(x, g, eps):
    ms = jnp.mean(x * x, axis=-1, keepdims=True)
    return x * lax.rsqrt(ms + eps) * g


def _const_spec(shape):
    nd = len(shape)
    return pl.BlockSpec(shape, lambda *_: (0,) * nd, pipeline_mode=pl.Buffered(1))


def _qkv_kernel(x_ref, g_ref, wqT_ref, wk_ref, wvT_ref, qT_ref, k_ref, vT_ref):
    hn = _rms(x_ref[0], g_ref[...], NORM_EPS).astype(BF16)
    qT_ref[0, 0] = lax.dot_general(wqT_ref[...], hn, NT_DIMS,
                                   preferred_element_type=F32).astype(BF16)
    k_ref[0] = jnp.dot(hn, wk_ref[...], preferred_element_type=F32).astype(BF16)
    vT_ref[0, 0] = lax.dot_general(wvT_ref[...], hn, NT_DIMS,
                                   preferred_element_type=F32).astype(BF16)


def _qkv(x, g, wqT, wk, wvT, tm):
    B, S, D = x.shape
    n = S // tm
    return pl.pallas_call(
        _qkv_kernel,
        grid=(B, n),
        in_specs=[
            pl.BlockSpec((1, tm, D), lambda b, i: (b, i, 0)),
            _const_spec((1, D)),
            _const_spec((D, D)),
            _const_spec((D, D)),
            _const_spec((D, D)),
        ],
        out_specs=[
            pl.BlockSpec((1, 1, D, tm), lambda b, i: (b, i, 0, 0)),
            pl.BlockSpec((1, tm, D), lambda b, i: (b, i, 0)),
            pl.BlockSpec((1, 1, D, tm), lambda b, i: (b, i, 0, 0)),
        ],
        out_shape=[
            jax.ShapeDtypeStruct((B, n, D, tm), BF16),
            jax.ShapeDtypeStruct((B, S, D), BF16),
            jax.ShapeDtypeStruct((B, n, D, tm), BF16),
        ],
        compiler_params=pltpu.CompilerParams(
            dimension_semantics=("arbitrary", "arbitrary"),
            vmem_limit_bytes=_vmem_limit(40 * 1024 * 1024)),
        name="qkv",
    )(x, g, wqT, wk, wvT)


def _t5_bucket_np(n):
    nf = np.maximum(n, 1).astype(np.float32)
    large = MAX_EXACT + (np.log(nf / np.float32(MAX_EXACT))
                         / np.float32(math.log(MAX_DISTANCE / MAX_EXACT))
                         * np.float32(NUM_BUCKETS - MAX_EXACT)).astype(np.int32)
    large = np.minimum(large, NUM_BUCKETS - 1)
    return np.where(n < MAX_EXACT, n, large)


def _bias_kernel(w_ref, out_ref, *, t):
    full = jnp.broadcast_to(w_ref[0], (t, 4 * t))
    c = pltpu.roll(full, 0, axis=1, stride=1, stride_axis=0)
    out_ref[0, 0] = c[:, :t]
    out_ref[0, 1] = c[:, t:2 * t]


def _bias_tiles(rel_bias, t):
    assert t >= MAX_DISTANCE and _t5_bucket_np(np.array([t + 1]))[0] == NUM_BUCKETS - 1
    H = rel_bias.shape[1]
    rb = rel_bias.astype(F32) - rel_bias[NUM_BUCKETS - 1].astype(F32)[None, :]
    near = jnp.take(rb, jnp.asarray(_t5_bucket_np(np.arange(2 * t))), axis=0).T
    w = jnp.concatenate([near, jnp.full((H, 2 * t), NEG_INF, F32)], axis=1) * LOG2E
    return pl.pallas_call(
        functools.partial(_bias_kernel, t=t),
        grid=(H,),
        in_specs=[pl.BlockSpec((1, 1, 4 * t), lambda h: (h, 0, 0))],
        out_specs=pl.BlockSpec((1, 2, t, t), lambda h: (h, 0, 0, 0)),
        out_shape=jax.ShapeDtypeStruct((H, 2, t, t), F32),
        compiler_params=pltpu.CompilerParams(dimension_semantics=("arbitrary",)),
        name="bias",
    )(w.reshape(H, 1, 4 * t))


def _attn_kernel(lqk_ref, qT_ref, k_ref, vT_ref, bias_ref, g_ref, o_ref,
                 q_scr, sa_scr, sb_scr, sc_scr, mba_scr, mbb_scr, mbc_scr, m_scr, acc_scr,
                 *, t, n):
    row = lax.broadcasted_iota(jnp.int32, (V_DIM, t), 0)
    for i in range(n):
        qT = qT_ref[0, i]
        zero = jnp.zeros_like(qT)
        q_scr[i, 0] = jnp.where(row < HEAD_DIM, qT, zero)
        q_scr[i, 1] = jnp.where(row >= HEAD_DIM, qT, zero)

    lqk = lqk_ref[...]
    e1 = jnp.exp(jnp.sum(lqk[0:1] * lqk[1:2], axis=-1, keepdims=True))
    e2 = jnp.exp(jnp.sum(lqk[2:3] * lqk[3:4], axis=-1, keepdims=True))
    lam = e1 - e2 + LAM_INIT_0

    def reset():
        m_scr[...] = jnp.full(m_scr.shape, NEG_INF, F32)
        acc_scr[...] = jnp.zeros(acc_scr.shape, F32)

    def scores(qi, j, bias_idx, s_scr, mb_scr):
        kc = k_ref[0, pl.ds(pl.multiple_of(j * t, t), t), :]
        for c in range(2):
            s = jnp.dot(kc, q_scr[qi, c], preferred_element_type=F32)
            if bias_idx is not None:
                s = s + bias_ref[0, bias_idx]
            s_scr[c] = s
            mb_scr[c] = jnp.max(s, axis=0, keepdims=True)

    ones_rows = jnp.where(lax.broadcasted_iota(jnp.int32, (BF16_ROWS, t), 0) == 0, 1.0, 0.0).astype(BF16)

    def consume(j, s_scr, mb_scr):
        vt = jnp.concatenate([vT_ref[0, j], ones_rows], axis=0)
        for c in range(2):
            m_old = m_scr[c]
            m_new = jnp.maximum(m_old, mb_scr[c])
            alpha = jnp.exp2(m_old - m_new)
            m_scr[c] = m_new
            p = jnp.exp2(s_scr[c] - m_new).astype(BF16)
            acc_scr[c] = alpha * acc_scr[c] + jnp.dot(vt, p, preferred_element_type=F32)

    def finalize(qi):
        l0 = acc_scr[0, V_DIM:V_DIM + 1, :]
        l1 = acc_scr[1, V_DIM:V_DIM + 1, :]
        o = acc_scr[0, :V_DIM, :] / l0 - lam * (acc_scr[1, :V_DIM, :] / l1)
        ms = jnp.mean(o * o, axis=0, keepdims=True)
        on = o * lax.rsqrt(ms + SUBLN_EPS) * g_ref[...] * (1.0 - LAM_INIT_0)
        o_ref[0, pl.ds(pl.multiple_of(qi * t, t), t), :] = on.T.astype(BF16)

    a_bufs = (sa_scr, mba_scr)
    b_bufs = (sb_scr, mbb_scr)
    c_bufs = (sc_scr, mbc_scr)

    reset()
    scores(0, 0, 0, *c_bufs)
    consume(0, *c_bufs)
    finalize(0)
    if n == 1:
        return
    scores(1, 1, 0, *c_bufs)

    def tile(qi, carry):
        reset()
        scores(qi, qi - 1, 1, *a_bufs)
        consume(qi, *c_bufs)
        n_far = qi - 1

        def pair(pp, c2):
            j0 = qi - 2 - 2 * pp
            scores(qi, j0, None, *b_bufs)
            consume(j0 + 1, *a_bufs)
            scores(qi, j0 - 1, None, *a_bufs)
            consume(j0, *b_bufs)
            return c2

        lax.fori_loop(0, n_far // 2, pair, 0)
        nxt = jnp.minimum(qi + 1, n - 1)

        @pl.when(n_far % 2 == 1)
        def _():
            scores(qi, 0, None, *b_bufs)
            consume(1, *a_bufs)
            scores(nxt, nxt, 0, *c_bufs)
            consume(0, *b_bufs)

        @pl.when(n_far % 2 == 0)
        def _():
            scores(nxt, nxt, 0, *c_bufs)
            consume(0, *a_bufs)

        finalize(qi)
        return carry

    lax.fori_loop(1, n, tile, 0)


def _attention(lqk, qT, k, vT, bias, g_col, t):
    B, n, D, _ = qT.shape
    S = n * t
    H = D // V_DIM
    kern = functools.partial(_attn_kernel, t=t, n=n)
    return pl.pallas_call(
        kern,
        grid=(B, H),
        in_specs=[
            _const_spec(lqk.shape),
            pl.BlockSpec((1, n, V_DIM, t), lambda b, h: (b, 0, h, 0)),
            pl.BlockSpec((1, S, V_DIM), lambda b, h: (b, 0, h)),
            pl.BlockSpec((1, n, V_DIM, t), lambda b, h: (b, 0, h, 0)),
            pl.BlockSpec((1, 2, t, t), lambda b, h: (h, 0, 0, 0)),
            _const_spec((V_DIM, 1)),
        ],
        out_specs=pl.BlockSpec((1, S, V_DIM), lambda b, h: (b, 0, h)),
        out_shape=jax.ShapeDtypeStruct((B, S, D), BF16),
        scratch_shapes=[
            pltpu.VMEM((n, 2, V_DIM, t), BF16),
            pltpu.VMEM((2, t, t), F32),
            pltpu.VMEM((2, t, t), F32),
            pltpu.VMEM((2, t, t), F32),
            pltpu.VMEM((2, 1, t), F32),
            pltpu.VMEM((2, 1, t), F32),
            pltpu.VMEM((2, 1, t), F32),
            pltpu.VMEM((2, 1, t), F32),
            pltpu.VMEM((2, V_DIM + BF16_ROWS, t), F32),
        ],
        compiler_params=pltpu.CompilerParams(
            dimension_semantics=("arbitrary", "arbitrary"),
            vmem_limit_bytes=_vmem_limit(52 * 1024 * 1024)),
        name="attn",
    )(lqk, qT, k, vT, bias, g_col)


def _swiglu_acc(h, wg, wu, wd, acc):
    width = wg.shape[-1]
    step = FF_SLAB if width % FF_SLAB == 0 else width
    for c in range(width // step):
        sl = slice(c * step, (c + 1) * step)
        gt = jnp.dot(h, wg[:, sl], preferred_element_type=F32)
        up = jnp.dot(h, wu[:, sl], preferred_element_type=F32)
        a = (gt * jax.nn.sigmoid(gt) * up).astype(BF16)
        acc = acc + jnp.dot(a, wd[sl, :], preferred_element_type=F32)
    return acc


def _ffn_kernel(o_ref, x_ref, wo_ref, g1_ref, wg_ref, wu_ref, wd_ref, g2_ref, x2_ref, h2_ref):
    x1 = x_ref[...] + jnp.dot(o_ref[...], wo_ref[...], preferred_element_type=F32)
    h1 = _rms(x1, g1_ref[...], NORM_EPS).astype(BF16)
    x2 = _swiglu_acc(h1, wg_ref, wu_ref, wd_ref, x1)
    x2_ref[...] = x2
    h2_ref[...] = _rms(x2, g2_ref[...], NORM_EPS).astype(BF16)


def _ffn(o, x, wo, g1, wg, wu, wd, g2, tm):
    T, D = x.shape
    Fd = wg.shape[1]
    kern = _ffn_kernel
    tile = lambda i: (i, 0)
    return pl.pallas_call(
        kern,
        grid=(T // tm,),
        in_specs=[
            pl.BlockSpec((tm, D), tile),
            pl.BlockSpec((tm, D), tile),
            _const_spec((D, D)),
            _const_spec((1, D)),
            _const_spec((D, Fd)),
            _const_spec((D, Fd)),
            _const_spec((Fd, D)),
            _const_spec((1, D)),
        ],
        out_specs=[pl.BlockSpec((tm, D), tile), pl.BlockSpec((tm, D), tile)],
        out_shape=[jax.ShapeDtypeStruct((T, D), F32), jax.ShapeDtypeStruct((T, D), BF16)],
        compiler_params=pltpu.CompilerParams(
            dimension_semantics=("arbitrary",),
            vmem_limit_bytes=_vmem_limit(56 * 1024 * 1024)),
        name="ffn",
    )(o, x, wo, g1, wg, wu, wd, g2)


def _gelu_tanh(x):
    return 0.5 * x * (1.0 + jnp.tanh(math.sqrt(2.0 / math.pi) * (x + 0.044715 * (x * x * x))))


def _lru_kernel(h2_ref, x2_ref, win_ref, cw_ref, cb_ref, gaw_ref, gab_ref, gxw_ref, gxb_ref,
                lam_ref, wout_ref, gn_ref, rT_ref, x3_ref, lg_ref,
                xbuf, a_scr, b_scr, h_scr, hcar, *, ts, dr):
    si = pl.program_id(1)
    halo = SUBLANES

    @pl.when(si == 0)
    def _():
        xbuf[0:halo, :] = jnp.zeros((halo, dr), F32)
        hcar[...] = jnp.zeros(hcar.shape, F32)

    gx = jnp.dot(h2_ref[0], win_ref[...], preferred_element_type=F32)
    gate = _gelu_tanh(gx[:, :dr])
    xb = gx[:, dr:]
    xbuf[halo:halo + ts, :] = xb

    cw = cw_ref[...]
    xc = cb_ref[...] + cw[CONV_WIDTH - 1:CONV_WIDTH] * xb
    for j in range(CONV_WIDTH - 1):
        back = CONV_WIDTH - 1 - j
        xc = xc + cw[j:j + 1] * xbuf[halo - back:halo - back + ts, :]
    xbuf[0:halo, :] = xb[ts - halo:ts, :]

    xcb = xc.astype(BF16)
    r_parts, i_parts = [], []
    for n in range(dr // LRU_BLOCK):
        blk = xcb[:, n * LRU_BLOCK:(n + 1) * LRU_BLOCK]
        r_parts.append(jnp.dot(blk, gaw_ref[n], preferred_element_type=F32))
        i_parts.append(jnp.dot(blk, gxw_ref[n], preferred_element_type=F32))
    r = jax.nn.sigmoid(jnp.concatenate(r_parts, axis=1) + gab_ref[...])
    ig = jax.nn.sigmoid(jnp.concatenate(i_parts, axis=1) + gxb_ref[...])
    z = -lam_ref[...]
    ez = jnp.exp(-jnp.abs(z))
    u = 1.0 + ez
    du = u - 1.0
    log1p_ez = jnp.where(du == 0.0, ez, jnp.log(u) * (ez / du))
    softplus = jnp.maximum(z, 0.0) + log1p_ez
    log_a = (-LRU_C) * r * softplus
    a = jnp.exp(log_a)
    th = jnp.tanh(log_a)
    b = jnp.sqrt(-2.0 * th / (1.0 - th)) * (ig * xc)

    ng = ts // SUBLANES
    a3 = a.reshape(ng, SUBLANES, dr)
    b3 = b.reshape(ng, SUBLANES, dr)
    sub = lax.broadcasted_iota(jnp.int32, a3.shape, 1)
    d = 1
    while d < SUBLANES:
        keep = sub >= d
        a_s = pltpu.roll(a3, d, axis=1)
        b_s = pltpu.roll(b3, d, axis=1)
        b3 = jnp.where(keep, a3 * b_s + b3, b3)
        a3 = jnp.where(keep, a3 * a_s, a3)
        d *= 2
    a_scr[...] = a3.reshape(ts, dr)
    b_scr[...] = b3.reshape(ts, dr)

    def group(g, hprev):
        r0 = pl.multiple_of(g * SUBLANES, SUBLANES)
        hg = a_scr[pl.ds(r0, SUBLANES), :] * hprev + b_scr[pl.ds(r0, SUBLANES), :]
        h_scr[pl.ds(r0, SUBLANES), :] = hg
        return jnp.broadcast_to(hg[SUBLANES - 1:SUBLANES, :], (SUBLANES, dr))

    hcar[...] = lax.fori_loop(0, ng, group, hcar[...], unroll=4)

    y = (h_scr[...] * gate).astype(BF16)
    x3 = x2_ref[0] + jnp.dot(y, wout_ref[...], preferred_element_type=F32)
    x3_ref[0] = x3

    h3 = _rms(x3, gn_ref[...], NORM_EPS)
    h_hi = h3.astype(BF16)
    h_lo = (h3 - h_hi.astype(F32)).astype(BF16)
    rT = rT_ref[...]
    r_hi = rT.astype(BF16)
    r_lo = (rT - r_hi.astype(F32)).astype(BF16)
    lg = lax.dot_general(r_hi, h_hi, NT_DIMS, preferred_element_type=F32)
    lg = lg + lax.dot_general(r_lo, h_hi, NT_DIMS, preferred_element_type=F32)
    lg = lg + lax.dot_general(r_hi, h_lo, NT_DIMS, preferred_element_type=F32)
    lg_ref[0] = lg


def _lru(h2, x2, win, cw, cb, gaw, gab, gxw, gxb, lam, wout, gn, rT, ts):
    B, S, D = x2.shape
    dr = wout.shape[0]
    E = rT.shape[0]
    kern = functools.partial(_lru_kernel, ts=ts, dr=dr)
    tile = lambda b, i: (b, i, 0)
    return pl.pallas_call(
        kern,
        grid=(B, S // ts),
        in_specs=[
            pl.BlockSpec((1, ts, D), tile),
            pl.BlockSpec((1, ts, D), tile),
            _const_spec(win.shape),
            _const_spec(cw.shape),
            _const_spec(cb.shape),
            _const_spec(gaw.shape),
            _const_spec(gab.shape),
            _const_spec(gxw.shape),
            _const_spec(gxb.shape),
            _const_spec(lam.shape),
            _const_spec(wout.shape),
            _const_spec(gn.shape),
            _const_spec(rT.shape),
        ],
        out_specs=[pl.BlockSpec((1, ts, D), tile),
                   pl.BlockSpec((1, E, ts), lambda b, i: (b, 0, i))],
        out_shape=[jax.ShapeDtypeStruct((B, S, D), F32),
                   jax.ShapeDtypeStruct((B, E, S), F32)],
        scratch_shapes=[
            pltpu.VMEM((SUBLANES + ts, dr), F32),
            pltpu.VMEM((ts, dr), F32),
            pltpu.VMEM((ts, dr), F32),
            pltpu.VMEM((ts, dr), F32),
            pltpu.VMEM((SUBLANES, dr), F32),
        ],
        compiler_params=pltpu.CompilerParams(
            dimension_semantics=("arbitrary", "arbitrary"),
            vmem_limit_bytes=_vmem_limit(48 * 1024 * 1024)),
        name="lru",
    )(h2, x2, win, cw, cb, gaw, gab, gxw, gxb, lam, wout, gn, rT)


def _route_kernel(lg_ref, idx_ref, gate_ref, cnt_ref, carry, *, tr):
    @pl.when(pl.program_id(0) == 0)
    def _():
        carry[...] = jnp.zeros(carry.shape, F32)

    lg = lg_ref[...]
    E = lg.shape[0]
    eidx = lax.broadcasted_iota(jnp.int32, lg.shape, 0).astype(F32)
    m1 = jnp.max(lg, axis=0, keepdims=True)
    i1 = jnp.min(jnp.where(lg == m1, eidx, float(E)), axis=0, keepdims=True)
    oh1 = eidx == i1
    lg2 = jnp.where(oh1, -jnp.inf, lg)
    m2 = jnp.max(lg2, axis=0, keepdims=True)
    i2 = jnp.min(jnp.where(lg2 == m2, eidx, float(E)), axis=0, keepdims=True)
    oh2 = eidx == i2
    ex = jnp.exp(m2 - m1)
    g1 = 1.0 / (1.0 + ex)
    g2 = ex / (1.0 + ex)

    both = jnp.where(oh1 | oh2, 1.0, 0.0).astype(BF16)
    ti = lax.broadcasted_iota(jnp.int32, (tr, tr), 0)
    tj = lax.broadcasted_iota(jnp.int32, (tr, tr), 1)
    before = jnp.where(ti < tj, 1.0, 0.0).astype(BF16)
    rank = jnp.dot(both, before, preferred_element_type=F32) + carry[...][:, 0:1]
    r1 = jnp.sum(jnp.where(oh1, rank, 0.0), axis=0, keepdims=True)
    r2 = jnp.sum(jnp.where(oh2, rank, 0.0), axis=0, keepdims=True)
    tot = carry[...] + jnp.sum(both.astype(F32), axis=1, keepdims=True)
    carry[...] = tot
    cnt_ref[...] = tot.astype(jnp.int32)

    orow = lax.broadcasted_iota(jnp.int32, (8, tr), 0)
    packed = jnp.where(orow == 0, i1, jnp.where(orow == 1, i2, jnp.where(orow == 2, r1, r2)))
    idx_ref[...] = packed.astype(jnp.int32)
    gate_ref[...] = jnp.where(orow == 0, g1, g2)


def _route(lgT, tr):
    E, T = lgT.shape
    kern = functools.partial(_route_kernel, tr=tr)
    return pl.pallas_call(
        kern,
        grid=(T // tr,),
        in_specs=[pl.BlockSpec((E, tr), lambda i: (0, i))],
        out_specs=[pl.BlockSpec((8, tr), lambda i: (0, i)),
                   pl.BlockSpec((8, tr), lambda i: (0, i)),
                   pl.BlockSpec((E, 128), lambda i: (0, 0))],
        out_shape=[jax.ShapeDtypeStruct((8, T), jnp.int32),
                   jax.ShapeDtypeStruct((8, T), F32),
                   jax.ShapeDtypeStruct((E, 128), jnp.int32)],
        scratch_shapes=[pltpu.VMEM((E, 128), F32)],
        compiler_params=pltpu.CompilerParams(dimension_semantics=("arbitrary",)),
        name="route",
    )(lgT)


def _dispatch_kernel(zt_ref, pos_ref, x_ref, xs_out, zero_scr, sem, zsem, *, td, rm):

    @pl.when(pl.program_id(0) == 0)
    def _():
        zero_scr[...] = jnp.zeros(zero_scr.shape, F32)
        for i in range(zt_ref.shape[0]):
            dst = xs_out.at[pl.ds(pl.multiple_of(zt_ref[i] * rm, rm), rm)]
            cp = pltpu.make_async_copy(zero_scr, dst, zsem)
            cp.start()
            cp.wait()

    def row_copy(g, u, k, dst_row):
        return pltpu.make_async_copy(x_ref.at[g, pl.ds(u, 1)], xs_out.at[pl.ds(dst_row, 1)], sem)

    def issue(g, c):
        for u in range(SUBLANES):
            for k in range(2):
                row_copy(g, u, k, pos_ref[g, 2 * u + k]).start(priority=k)
        return c

    lax.fori_loop(0, td // SUBLANES, issue, 0)

    def drain(g, c):
        for u in range(SUBLANES):
            for k in range(2):
                row_copy(0, 0, k, 0).wait()
        return c

    lax.fori_loop(0, td // SUBLANES, drain, 0)


def _dispatch(zero_tiles, pos_g, x, n_rows, td, rm):
    T, D = x.shape
    kern = functools.partial(_dispatch_kernel, td=td, rm=rm)
    gd = td // SUBLANES
    return pl.pallas_call(
        kern,
        grid_spec=pltpu.PrefetchScalarGridSpec(
            num_scalar_prefetch=1,
            grid=(T // td,),
            in_specs=[
                pl.BlockSpec((gd, 2 * SUBLANES), lambda i, zt: (i, 0), memory_space=pltpu.SMEM),
                pl.BlockSpec((gd, SUBLANES, D), lambda i, zt: (i, 0, 0)),
            ],
            out_specs=pl.BlockSpec(memory_space=pl.ANY),
            scratch_shapes=[pltpu.VMEM((rm, D), F32), pltpu.SemaphoreType.DMA,
                            pltpu.SemaphoreType.DMA],
        ),
        out_shape=jax.ShapeDtypeStruct((n_rows, D), F32),
        compiler_params=pltpu.CompilerParams(
            dimension_semantics=("arbitrary",), has_side_effects=True),
        name="dispatch",
    )(zero_tiles, pos_g, x.reshape(T // SUBLANES, SUBLANES, D))


def _expert_kernel(te_ref, nu_ref, xs_ref, gn_ref, wg_ref, wu_ref, wd_ref, ys_ref,
                   h_scr, acc_scr, *, nf):
    i = pl.program_id(0)
    j = pl.program_id(1)

    @pl.when(i < nu_ref[0])
    def _():
        @pl.when(j == 0)
        def _():
            h_scr[...] = _rms(xs_ref[...], gn_ref[...], NORM_EPS).astype(BF16)
            acc_scr[...] = jnp.zeros(acc_scr.shape, F32)

        acc_scr[...] = _swiglu_acc(h_scr[...], wg_ref.at[0], wu_ref.at[0], wd_ref.at[0],
                                   acc_scr[...])

        @pl.when(j == nf - 1)
        def _():
            ys_ref[...] = acc_scr[...]

    @pl.when((i >= nu_ref[0]) & (j == nf - 1))
    def _():
        ys_ref[...] = jnp.zeros(ys_ref.shape, F32)


def _experts(tile_expert, n_used, xs, gn, wg, wu, wd, tm, nf):
    NP, D = xs.shape
    E, _, Fd = wg.shape
    fc = Fd // nf
    nt = NP // tm

    def eff(i, nu):
        return jnp.minimum(i, nu[0] - 1)

    def chunk(i, j, nu):
        ie = eff(i, nu)
        je = jnp.where(i < nu[0], j, nf - 1)
        return jnp.where(ie % 2 == 0, je, nf - 1 - je)

    kern = functools.partial(_expert_kernel, nf=nf)
    return pl.pallas_call(
        kern,
        grid_spec=pltpu.PrefetchScalarGridSpec(
            num_scalar_prefetch=2,
            grid=(nt, nf),
            in_specs=[
                pl.BlockSpec((tm, D), lambda i, j, te, nu: (eff(i, nu), 0)),
                pl.BlockSpec((1, D), lambda i, j, te, nu: (0, 0)),
                pl.BlockSpec((1, D, fc), lambda i, j, te, nu: (te[eff(i, nu)], 0, chunk(i, j, nu))),
                pl.BlockSpec((1, D, fc), lambda i, j, te, nu: (te[eff(i, nu)], 0, chunk(i, j, nu))),
                pl.BlockSpec((1, fc, D), lambda i, j, te, nu: (te[eff(i, nu)], chunk(i, j, nu), 0)),
            ],
            out_specs=pl.BlockSpec((tm, D), lambda i, j, te, nu: (i, 0)),
            scratch_shapes=[pltpu.VMEM((tm, D), BF16), pltpu.VMEM((tm, D), F32)],
        ),
        out_shape=jax.ShapeDtypeStruct((NP, D), F32),
        compiler_params=pltpu.CompilerParams(
            dimension_semantics=("arbitrary", "arbitrary"),
            vmem_limit_bytes=_vmem_limit(56 * 1024 * 1024)),
        name="experts",
    )(tile_expert, n_used, xs, gn, wg, wu, wd)


def _combine_kernel(pos_ref, gate_ref, x_ref, gn_ref, ys_hbm, out_ref, buf, sem, *, tc):
    def row_copy(g, u, k, src_row):
        return pltpu.make_async_copy(ys_hbm.at[pl.ds(src_row, 1)], buf.at[k, g, pl.ds(u, 1)], sem)

    def issue(g, c):
        for u in range(SUBLANES):
            for k in range(2):
                row_copy(g, u, k, pos_ref[g, 2 * u + k]).start(priority=k)
        return c

    lax.fori_loop(0, tc // SUBLANES, issue, 0)

    def drain(g, c):
        for u in range(SUBLANES):
            for k in range(2):
                row_copy(0, 0, k, 0).wait()
        return c

    lax.fori_loop(0, tc // SUBLANES, drain, 0)

    g = gate_ref[...]
    D = x_ref.shape[-1]
    x4 = (x_ref[...] + g[:, 0:1] * buf[0].reshape(tc, D) + g[:, 1:2] * buf[1].reshape(tc, D))
    out_ref[...] = _rms(x4, gn_ref[...], NORM_EPS)


def _combine(pos_g, gates_col, x, gn, ys, tc):
    T, D = x.shape
    kern = functools.partial(_combine_kernel, tc=tc)
    gc = tc // SUBLANES
    return pl.pallas_call(
        kern,
        grid=(T // tc,),
        in_specs=[
            pl.BlockSpec((gc, 2 * SUBLANES), lambda i: (i, 0), memory_space=pltpu.SMEM),
            pl.BlockSpec((tc, 2), lambda i: (i, 0)),
            pl.BlockSpec((tc, D), lambda i: (i, 0)),
            _const_spec((1, D)),
            pl.BlockSpec(memory_space=pl.ANY),
        ],
        out_specs=pl.BlockSpec((tc, D), lambda i: (i, 0)),
        out_shape=jax.ShapeDtypeStruct((T, D), F32),
        scratch_shapes=[pltpu.VMEM((2, gc, SUBLANES, D), F32), pltpu.SemaphoreType.DMA],
        compiler_params=pltpu.CompilerParams(
            dimension_semantics=("arbitrary",),
            vmem_limit_bytes=_vmem_limit(40 * 1024 * 1024)),
        name="combine",
    )(pos_g, gates_col, x, gn, ys)


def _tile(n, pref):
    t = min(n, pref)
    assert n % t == 0, (n, t)
    return t


def kernel(x, norm_mix, norm_ffn, norm_final, rel_bias, attn_w_qkv, attn_w_o, attn_lambda_q1, attn_lambda_k1, attn_lambda_q2, attn_lambda_k2, attn_subln, lru_w_in, lru_conv_w, lru_conv_b, lru_gate_a_w, lru_gate_a_b, lru_gate_x_w, lru_gate_x_b, lru_lambda, lru_w_out, ffn_w_gate, ffn_w_up, ffn_w_down, moe_router, moe_w_gate, moe_w_up, moe_w_down):
    B, S, D = x.shape
    T = B * S
    assert D == N_HEADS * V_DIM
    row = lambda v: v.reshape(1, -1).astype(F32)

    ta = _tile(S, ATTN_TILE)
    wqkv = attn_w_qkv[0]
    wqT = (wqkv[:, :D] * (SCALE * LOG2E)).T.astype(BF16)
    wk = wqkv[:, D:2 * D].astype(BF16)
    wvT = wqkv[:, 2 * D:].T.astype(BF16)
    qT, k, vT = _qkv(x, row(norm_mix[0]), wqT, wk, wvT, ta)
    lqk = jnp.stack([attn_lambda_q1[0], attn_lambda_k1[0],
                     attn_lambda_q2[0], attn_lambda_k2[0]]).astype(F32)
    bias = _bias_tiles(rel_bias, ta)
    o = _attention(lqk, qT, k, vT, bias, attn_subln[0].reshape(V_DIM, 1).astype(F32), ta)

    tm = _tile(T, TOKEN_TILE)
    x2, h2 = _ffn(o.reshape(T, D), x.reshape(T, D), attn_w_o[0].astype(BF16), row(norm_ffn[0]),
                  ffn_w_gate[0].astype(BF16), ffn_w_up[0].astype(BF16),
                  ffn_w_down[0].astype(BF16), row(norm_mix[1]), tm)

    ts = _tile(S, LRU_TILE)
    x3, lg = _lru(h2.reshape(B, S, D), x2.reshape(B, S, D), lru_w_in[0].astype(BF16),
                  lru_conv_w[0].astype(F32), row(lru_conv_b[0]),
                  lru_gate_a_w[0].astype(BF16), row(lru_gate_a_b[0]),
                  lru_gate_x_w[0].astype(BF16), row(lru_gate_x_b[0]),
                  row(lru_lambda[0]), lru_w_out[0].astype(BF16), row(norm_ffn[1]),
                  moe_router[0].T.astype(F32), ts)
    x3 = x3.reshape(T, D)
    E = moe_router.shape[-1]
    lgT = jnp.transpose(lg, (1, 0, 2)).reshape(E, T)

    idx, gates, counts = _route(lgT, _tile(T, ROUTE_TILE))
    rm = EXPERT_ROW_TILE
    cnt = counts[:, 0]
    padded = ((cnt + rm - 1) // rm) * rm
    ends = jnp.cumsum(padded)
    starts = ends - padded
    pos = jnp.stack([starts[idx[0]] + idx[2], starts[idx[1]] + idx[3]])
    n_tiles = (2 * T) // rm + E
    tile_rows = jnp.arange(n_tiles, dtype=jnp.int32) * rm
    tile_expert = jnp.minimum(
        jnp.sum((tile_rows[:, None] >= ends[None, :]).astype(jnp.int32), axis=1), E - 1)
    n_used = (ends[-1] // rm).astype(jnp.int32).reshape(1)

    last_tile = jnp.maximum(ends // rm - 1, 0)
    tail_tile = jnp.minimum(n_used[0] + jnp.arange(E, dtype=jnp.int32), n_tiles - 1)
    zero_tiles = jnp.concatenate([last_tile, tail_tile]).astype(jnp.int32)

    td = _tile(T, DISPATCH_TILE)
    pos_g = pos.T.reshape(T // SUBLANES, 2 * SUBLANES)
    xs = _dispatch(zero_tiles, pos_g, x3, n_tiles * rm, td, rm)
    ys = _experts(tile_expert, n_used, xs, row(norm_ffn[1]), moe_w_gate[0].astype(BF16),
                  moe_w_up[0].astype(BF16), moe_w_down[0].astype(BF16), rm, EXPERT_FF_CHUNKS)
    out = _combine(pos_g, gates[:2].T, x3, row(norm_final), ys, td)
    return out.reshape(B, S, D)
```

```python
import functools
import math

import numpy as np
import jax
import jax.numpy as jnp
from jax import lax
from jax.experimental import pallas as pl
from jax.experimental.pallas import tpu as pltpu

F32 = jnp.float32
BF16 = jnp.bfloat16

N_HEADS = 8
HEAD_DIM = 64
V_DIM = 2 * HEAD_DIM
SCALE = HEAD_DIM ** -0.5
LOG2E = math.log2(math.e)
NEG_INF = -1e30
NUM_BUCKETS = 32
MAX_EXACT = NUM_BUCKETS // 2
MAX_DISTANCE = 128
N_LRU_BLOCKS = 10
LRU_BLOCK = 128
CONV_WIDTH = 4
LRU_C = 8.0
N_EXPERTS = 8
NORM_EPS = 1e-6
SUBLN_EPS = 1e-5
LAM_INIT_0 = 0.8 - 0.6 * math.exp(-0.3 * 0)

V7X_VMEM_BYTES = 64 * 1024 * 1024
SUBLANES = 8
BF16_ROWS = 16

ATTN_TILE = 512
TOKEN_TILE = 512
LRU_TILE = 256
ROUTE_TILE = 1024
EXPERT_ROW_TILE = 512
EXPERT_FF_CHUNKS = 2
FF_SLAB = 256
PROJ_SLAB = 256
DISPATCH_TILE = 512

NT_DIMS = (((1,), (1,)), ((), ()))


def _vmem_limit(nbytes):
    return int(min(nbytes, V7X_VMEM_BYTES - 6 * 1024 * 1024))


def _rms(x, g, eps):
    ms = jnp.mean(x * x, axis=-1, keepdims=True)
    return x * lax.rsqrt(ms + eps) * g


def _const_spec(shape):
    nd = len(shape)
    return pl.BlockSpec(shape, lambda *_: (0,) * nd, pipeline_mode=pl.Buffered(1))


def _qkv_kernel(x_ref, g_ref, wqT_ref, wk_ref, wvT_ref, qT_ref, k_ref, vT_ref):
    hn = _rms(x_ref[0], g_ref[...], NORM_EPS).astype(BF16)
    qT_ref[0, 0] = lax.dot_general(wqT_ref[...], hn, NT_DIMS,
                                   preferred_element_type=F32).astype(BF16)
    k_ref[0] = jnp.dot(hn, wk_ref[...], preferred_element_type=F32).astype(BF16)
    vT_ref[0, 0] = lax.dot_general(wvT_ref[...], hn, NT_DIMS,
                                   preferred_element_type=F32).astype(BF16)


def _qkv(x, g, wqT, wk, wvT, tm):
    B, S, D = x.shape
    n = S // tm
    return pl.pallas_call(
        _qkv_kernel,
        grid=(B, n),
        in_specs=[
            pl.BlockSpec((1, tm, D), lambda b, i: (b, i, 0)),
            _const_spec((1, D)),
            _const_spec((D, D)),
            _const_spec((D, D)),
            _const_spec((D, D)),
        ],
        out_specs=[
            pl.BlockSpec((1, 1, D, tm), lambda b, i: (b, i, 0, 0)),
            pl.BlockSpec((1, tm, D), lambda b, i: (b, i, 0)),
            pl.BlockSpec((1, 1, D, tm), lambda b, i: (b, i, 0, 0)),
        ],
        out_shape=[
            jax.ShapeDtypeStruct((B, n, D, tm), BF16),
            jax.ShapeDtypeStruct((B, S, D), BF16),
            jax.ShapeDtypeStruct((B, n, D, tm), BF16),
        ],
        compiler_params=pltpu.CompilerParams(
            dimension_semantics=("arbitrary", "arbitrary"),
            vmem_limit_bytes=_vmem_limit(40 * 1024 * 1024)),
        name="qkv",
    )(x, g, wqT, wk, wvT)


def _t5_bucket_np(n):
    nf = np.maximum(n, 1).astype(np.float32)
    large = MAX_EXACT + (np.log(nf / np.float32(MAX_EXACT))
                         / np.float32(math.log(MAX_DISTANCE / MAX_EXACT))
                         * np.float32(NUM_BUCKETS - MAX_EXACT)).astype(np.int32)
    large = np.minimum(large, NUM_BUCKETS - 1)
    return np.where(n < MAX_EXACT, n, large)


def _bias_kernel(w_ref, out_ref, *, t):
    full = jnp.broadcast_to(w_ref[0], (t, 4 * t))
    c = pltpu.roll(full, 0, axis=1, stride=1, stride_axis=0)
    out_ref[0, 0] = c[:, :t]
    out_ref[0, 1] = c[:, t:2 * t]


def _bias_tiles(rel_bias, t):
    assert t >= MAX_DISTANCE and _t5_bucket_np(np.array([t + 1]))[0] == NUM_BUCKETS - 1
    H = rel_bias.shape[1]
    rb = rel_bias.astype(F32) - rel_bias[NUM_BUCKETS - 1].astype(F32)[None, :]
    near = jnp.take(rb, jnp.asarray(_t5_bucket_np(np.arange(2 * t))), axis=0).T
    w = jnp.concatenate([near, jnp.full((H, 2 * t), NEG_INF, F32)], axis=1) * LOG2E
    return pl.pallas_call(
        functools.partial(_bias_kernel, t=t),
        grid=(H,),
        in_specs=[pl.BlockSpec((1, 1, 4 * t), lambda h: (h, 0, 0))],
        out_specs=pl.BlockSpec((1, 2, t, t), lambda h: (h, 0, 0, 0)),
        out_shape=jax.ShapeDtypeStruct((H, 2, t, t), F32),
        compiler_params=pltpu.CompilerParams(dimension_semantics=("arbitrary",)),
        name="bias",
    )(w.reshape(H, 1, 4 * t))


def _attn_kernel(lqk_ref, qT_ref, k_ref, vT_ref, bias_ref, g_ref, o_ref,
                 q_scr, sa_scr, sb_scr, sc_scr, mba_scr, mbb_scr, mbc_scr, m_scr, acc_scr,
                 *, t, n):
    row = lax.broadcasted_iota(jnp.int32, (V_DIM, t), 0)
    for i in range(n):
        qT = qT_ref[0, i]
        zero = jnp.zeros_like(qT)
        q_scr[i, 0] = jnp.where(row < HEAD_DIM, qT, zero)
        q_scr[i, 1] = jnp.where(row >= HEAD_DIM, qT, zero)

    lqk = lqk_ref[...]
    e1 = jnp.exp(jnp.sum(lqk[0:1] * lqk[1:2], axis=-1, keepdims=True))
    e2 = jnp.exp(jnp.sum(lqk[2:3] * lqk[3:4], axis=-1, keepdims=True))
    lam = e1 - e2 + LAM_INIT_0

    def reset():
        m_scr[...] = jnp.full(m_scr.shape, NEG_INF, F32)
        acc_scr[...] = jnp.zeros(acc_scr.shape, F32)

    def scores(qi, j, bias_idx, s_scr, mb_scr):
        kc = k_ref[0, pl.ds(pl.multiple_of(j * t, t), t), :]
        for c in range(2):
            s = jnp.dot(kc, q_scr[qi, c], preferred_element_type=F32)
            if bias_idx is not None:
                s = s + bias_ref[0, bias_idx]
            s_scr[c] = s
            mb_scr[c] = jnp.max(s, axis=0, keepdims=True)

    ones_rows = jnp.where(lax.broadcasted_iota(jnp.int32, (BF16_ROWS, t), 0) == 0, 1.0, 0.0).astype(BF16)

    def consume(j, s_scr, mb_scr):
        vt = jnp.concatenate([vT_ref[0, j], ones_rows], axis=0)
        for c in range(2):
            m_old = m_scr[c]
            m_new = jnp.maximum(m_old, mb_scr[c])
            alpha = jnp.exp2(m_old - m_new)
            m_scr[c] = m_new
            p = jnp.exp2(s_scr[c] - m_new).astype(BF16)
            acc_scr[c] = alpha * acc_scr[c] + jnp.dot(vt, p, preferred_element_type=F32)

    def finalize(qi):
        l0 = acc_scr[0, V_DIM:V_DIM + 1, :]
        l1 = acc_scr[1, V_DIM:V_DIM + 1, :]
        o = acc_scr[0, :V_DIM, :] / l0 - lam * (acc_scr[1, :V_DIM, :] / l1)
        ms = jnp.mean(o * o, axis=0, keepdims=True)
        on = o * lax.rsqrt(ms + SUBLN_EPS) * g_ref[...] * (1.0 - LAM_INIT_0)
        o_ref[0, pl.ds(pl.multiple_of(qi * t, t), t), :] = on.T.astype(BF16)

    a_bufs = (sa_scr, mba_scr)
    b_bufs = (sb_scr, mbb_scr)
    c_bufs = (sc_scr, mbc_scr)

    reset()
    scores(0, 0, 0, *c_bufs)
    consume(0, *c_bufs)
    finalize(0)
    if n == 1:
        return
    scores(1, 1, 0, *c_bufs)

    def tile(qi, carry):
        reset()
        scores(qi, qi - 1, 1, *a_bufs)
        consume(qi, *c_bufs)
        n_far = qi - 1

        def pair(pp, c2):
            j0 = qi - 2 - 2 * pp
            scores(qi, j0, None, *b_bufs)
            consume(j0 + 1, *a_bufs)
            scores(qi, j0 - 1, None, *a_bufs)
            consume(j0, *b_bufs)
            return c2

        lax.fori_loop(0, n_far // 2, pair, 0)
        nxt = jnp.minimum(qi + 1, n - 1)

        @pl.when(n_far % 2 == 1)
        def _():
            scores(qi, 0, None, *b_bufs)
            consume(1, *a_bufs)
            scores(nxt, nxt, 0, *c_bufs)
            consume(0, *b_bufs)

        @pl.when(n_far % 2 == 0)
        def _():
            scores(nxt, nxt, 0, *c_bufs)
            consume(0, *a_bufs)

        finalize(qi)
        return carry

    lax.fori_loop(1, n, tile, 0)


def _attention(lqk, qT, k, vT, bias, g_col, t):
    B, n, D, _ = qT.shape
    S = n * t
    H = D // V_DIM
    kern = functools.partial(_attn_kernel, t=t, n=n)
    return pl.pallas_call(
        kern,
        grid=(B, H),
        in_specs=[
            _const_spec(lqk.shape),
            pl.BlockSpec((1, n, V_DIM, t), lambda b, h: (b, 0, h, 0)),
            pl.BlockSpec((1, S, V_DIM), lambda b, h: (b, 0, h)),
            pl.BlockSpec((1, n, V_DIM, t), lambda b, h: (b, 0, h, 0)),
            pl.BlockSpec((1, 2, t, t), lambda b, h: (h, 0, 0, 0)),
            _const_spec((V_DIM, 1)),
        ],
        out_specs=pl.BlockSpec((1, S, V_DIM), lambda b, h: (b, 0, h)),
        out_shape=jax.ShapeDtypeStruct((B, S, D), BF16),
        scratch_shapes=[
            pltpu.VMEM((n, 2, V_DIM, t), BF16),
            pltpu.VMEM((2, t, t), F32),
            pltpu.VMEM((2, t, t), F32),
            pltpu.VMEM((2, t, t), F32),
            pltpu.VMEM((2, 1, t), F32),
            pltpu.VMEM((2, 1, t), F32),
            pltpu.VMEM((2, 1, t), F32),
            pltpu.VMEM((2, 1, t), F32),
            pltpu.VMEM((2, V_DIM + BF16_ROWS, t), F32),
        ],
        compiler_params=pltpu.CompilerParams(
            dimension_semantics=("arbitrary", "arbitrary"),
            vmem_limit_bytes=_vmem_limit(52 * 1024 * 1024)),
        name="attn",
    )(lqk, qT, k, vT, bias, g_col)


def _swiglu_acc(h, wg, wu, wd, acc):
    width = wg.shape[-1]
    step = FF_SLAB if width % FF_SLAB == 0 else width
    for c in range(width // step):
        sl = slice(c * step, (c + 1) * step)
        gt = jnp.dot(h, wg[:, sl], preferred_element_type=F32)
        up = jnp.dot(h, wu[:, sl], preferred_element_type=F32)
        a = (gt * jax.nn.sigmoid(gt) * up).astype(BF16)
        acc = acc + jnp.dot(a, wd[sl, :], preferred_element_type=F32)
    return acc


def _ffn_kernel(o_ref, x_ref, wo_ref, g1_ref, wg_ref, wu_ref, wd_ref, g2_ref, x2_ref, h2_ref):
    x1 = x_ref[...] + jnp.dot(o_ref[...], wo_ref[...], preferred_element_type=F32)
    h1 = _rms(x1, g1_ref[...], NORM_EPS).astype(BF16)
    x2 = _swiglu_acc(h1, wg_ref, wu_ref, wd_ref, x1)
    x2_ref[...] = x2
    h2_ref[...] = _rms(x2, g2_ref[...], NORM_EPS).astype(BF16)


def _ffn(o, x, wo, g1, wg, wu, wd, g2, tm):
    T, D = x.shape
    Fd = wg.shape[1]
    kern = _ffn_kernel
    tile = lambda i: (i, 0)
    return pl.pallas_call(
        kern,
        grid=(T // tm,),
        in_specs=[
            pl.BlockSpec((tm, D), tile),
            pl.BlockSpec((tm, D), tile),
            _const_spec((D, D)),
            _const_spec((1, D)),
            _const_spec((D, Fd)),
            _const_spec((D, Fd)),
            _const_spec((Fd, D)),
            _const_spec((1, D)),
        ],
        out_specs=[pl.BlockSpec((tm, D), tile), pl.BlockSpec((tm, D), tile)],
        out_shape=[jax.ShapeDtypeStruct((T, D), F32), jax.ShapeDtypeStruct((T, D), BF16)],
        compiler_params=pltpu.CompilerParams(
            dimension_semantics=("arbitrary",),
            vmem_limit_bytes=_vmem_limit(56 * 1024 * 1024)),
        name="ffn",
    )(o, x, wo, g1, wg, wu, wd, g2)


def _gelu_tanh(x):
    return 0.5 * x * (1.0 + jnp.tanh(math.sqrt(2.0 / math.pi) * (x + 0.044715 * (x * x * x))))


def _lru_kernel(h2_ref, h2n_ref, x2_ref, win_ref, cw_ref, cb_ref, gaw_ref, gab_ref, gxw_ref,
                gxb_ref, lam_ref, wout_ref, gn_ref, rT_ref, x3_ref, h3_ref, lg_ref,
                gx0_scr, gx1_scr, xbuf, a_scr, b_scr, h_scr, hcar, *, ts, dr):
    si = pl.program_id(1)
    halo = SUBLANES

    @pl.when(si == 0)
    def _():
        xbuf[0:halo, :] = jnp.zeros((halo, dr), F32)
        hcar[...] = jnp.zeros(hcar.shape, F32)

    @pl.when((pl.program_id(0) == 0) & (si == 0))
    def _():
        gx0_scr[...] = jnp.dot(h2_ref[0, 0:ts, :], win_ref[...], preferred_element_type=F32)

    z = -lam_ref[...]
    ez = jnp.exp(-jnp.abs(z))
    u = 1.0 + ez
    du = u - 1.0
    log1p_ez = jnp.where(du == 0.0, ez, jnp.log(u) * (ez / du))
    softplus = jnp.maximum(z, 0.0) + log1p_ez

    def in_proj_slabs(load_rows, gx_scr):
        def slab(c0):
            def emit():
                gx_scr[:, c0:c0 + PROJ_SLAB] = jnp.dot(
                    load_rows(), win_ref[:, c0:c0 + PROJ_SLAB], preferred_element_type=F32)
            return emit
        return [slab(c0) for c0 in range(0, 2 * dr, PROJ_SLAB)]

    _lru_tile(0, gx0_scr, in_proj_slabs(lambda: h2_ref[0, ts:2 * ts, :], gx1_scr), softplus,
              x2_ref, cw_ref, cb_ref, gaw_ref, gab_ref, gxw_ref, gxb_ref,
              wout_ref, gn_ref, rT_ref, x3_ref, h3_ref, lg_ref, xbuf, a_scr, b_scr, h_scr, hcar,
              ts=ts, dr=dr)
    _lru_tile(ts, gx1_scr, in_proj_slabs(lambda: h2n_ref[0], gx0_scr), softplus,
              x2_ref, cw_ref, cb_ref, gaw_ref, gab_ref, gxw_ref, gxb_ref,
              wout_ref, gn_ref, rT_ref, x3_ref, h3_ref, lg_ref, xbuf, a_scr, b_scr, h_scr, hcar,
              ts=ts, dr=dr)


def _lru_tile(r0, gx_scr, background, softplus, x2_ref, cw_ref, cb_ref, gaw_ref, gab_ref, gxw_ref,
              gxb_ref, wout_ref, gn_ref, rT_ref, x3_ref, h3_ref, lg_ref, xbuf, a_scr, b_scr, h_scr,
              hcar, *, ts, dr):
    halo = SUBLANES
    pending = list(background)
    per_phase = -(-len(pending) // 5)

    def emit_background():
        for _ in range(min(per_phase, len(pending))):
            pending.pop(0)()

    emit_background()
    gate = _gelu_tanh(gx_scr[:, :dr])
    xb = gx_scr[:, dr:]
    xbuf[halo:halo + ts, :] = xb

    emit_background()
    cw = cw_ref[...]
    xc = cb_ref[...] + cw[CONV_WIDTH - 1:CONV_WIDTH] * xb
    for j in range(CONV_WIDTH - 1):
        back = CONV_WIDTH - 1 - j
        xc = xc + cw[j:j + 1] * xbuf[halo - back:halo - back + ts, :]
    xbuf[0:halo, :] = xb[ts - halo:ts, :]

    xcb = xc.astype(BF16)
    r_parts, i_parts = [], []
    for n in range(dr // LRU_BLOCK):
        blk = xcb[:, n * LRU_BLOCK:(n + 1) * LRU_BLOCK]
        r_parts.append(jnp.dot(blk, gaw_ref[n], preferred_element_type=F32))
        i_parts.append(jnp.dot(blk, gxw_ref[n], preferred_element_type=F32))
    r = jax.nn.sigmoid(jnp.concatenate(r_parts, axis=1) + gab_ref[...])
    ig = jax.nn.sigmoid(jnp.concatenate(i_parts, axis=1) + gxb_ref[...])
    emit_background()
    log_a = (-LRU_C) * r * softplus
    a = jnp.exp(log_a)
    b = jnp.sqrt(1.0 - a * a) * (ig * xc)

    emit_background()
    ng = ts // SUBLANES
    a3 = a.reshape(ng, SUBLANES, dr)
    b3 = b.reshape(ng, SUBLANES, dr)
    sub = lax.broadcasted_iota(jnp.int32, a3.shape, 1)
    d = 1
    while d < SUBLANES:
        keep = sub >= d
        a_s = pltpu.roll(a3, d, axis=1)
        b_s = pltpu.roll(b3, d, axis=1)
        b3 = jnp.where(keep, a3 * b_s + b3, b3)
        a3 = jnp.where(keep, a3 * a_s, a3)
        d *= 2
    a_scr[...] = a3.reshape(ts, dr)
    b_scr[...] = b3.reshape(ts, dr)

    emit_background()
    assert not pending
    hprev = hcar[...]
    for g in range(ng):
        rows = slice(g * SUBLANES, (g + 1) * SUBLANES)
        hg = a_scr[rows, :] * hprev + b_scr[rows, :]
        h_scr[rows, :] = hg
        hprev = jnp.broadcast_to(hg[SUBLANES - 1:SUBLANES, :], (SUBLANES, dr))
    hcar[...] = hprev

    y = (h_scr[...] * gate).astype(BF16)
    x3 = x2_ref[0, r0:r0 + ts, :] + jnp.dot(y, wout_ref[...], preferred_element_type=F32)
    x3_ref[0, r0:r0 + ts, :] = x3

    h3 = _rms(x3, gn_ref[...], NORM_EPS)
    h3_ref[0, r0:r0 + ts, :] = h3
    h_hi = h3.astype(BF16)
    h_lo = (h3 - h_hi.astype(F32)).astype(BF16)
    rT = rT_ref[...]
    r_hi = rT.astype(BF16)
    r_lo = (rT - r_hi.astype(F32)).astype(BF16)
    lg = lax.dot_general(r_hi, h_hi, NT_DIMS, preferred_element_type=F32)
    lg = lg + lax.dot_general(r_lo, h_hi, NT_DIMS, preferred_element_type=F32)
    lg = lg + lax.dot_general(r_hi, h_lo, NT_DIMS, preferred_element_type=F32)
    lg_ref[0, :, r0:r0 + ts] = lg


def _lru(h2, x2, win, cw, cb, gaw, gab, gxw, gxb, lam, wout, gn, rT, ts):
    B, S, D = x2.shape
    dr = wout.shape[0]
    E = rT.shape[0]
    n = S // (2 * ts)
    kern = functools.partial(_lru_kernel, ts=ts, dr=dr)
    tile = lambda b, i: (b, i, 0)

    def next_first(b, i):
        last = i + 1 >= n
        return (jnp.where(last, jnp.minimum(b + 1, B - 1), b), jnp.where(last, 0, 2 * (i + 1)), 0)

    return pl.pallas_call(
        kern,
        grid=(B, n),
        in_specs=[
            pl.BlockSpec((1, 2 * ts, D), tile),
            pl.BlockSpec((1, ts, D), next_first),
            pl.BlockSpec((1, 2 * ts, D), tile),
            _const_spec(win.shape),
            _const_spec(cw.shape),
            _const_spec(cb.shape),
            _const_spec(gaw.shape),
            _const_spec(gab.shape),
            _const_spec(gxw.shape),
            _const_spec(gxb.shape),
            _const_spec(lam.shape),
            _const_spec(wout.shape),
            _const_spec(gn.shape),
            _const_spec(rT.shape),
        ],
        out_specs=[pl.BlockSpec((1, 2 * ts, D), tile),
                   pl.BlockSpec((1, 2 * ts, D), tile),
                   pl.BlockSpec((1, E, 2 * ts), lambda b, i: (b, 0, i))],
        out_shape=[jax.ShapeDtypeStruct((B, S, D), F32),
                   jax.ShapeDtypeStruct((B, S, D), F32),
                   jax.ShapeDtypeStruct((B, E, S), F32)],
        scratch_shapes=[
            pltpu.VMEM((ts, 2 * dr), F32),
            pltpu.VMEM((ts, 2 * dr), F32),
            pltpu.VMEM((SUBLANES + ts, dr), F32),
            pltpu.VMEM((ts, dr), F32),
            pltpu.VMEM((ts, dr), F32),
            pltpu.VMEM((ts, dr), F32),
            pltpu.VMEM((SUBLANES, dr), F32),
        ],
        compiler_params=pltpu.CompilerParams(
            dimension_semantics=("arbitrary", "arbitrary"),
            vmem_limit_bytes=_vmem_limit(48 * 1024 * 1024)),
        name="lru",
    )(h2, h2, x2, win, cw, cb, gaw, gab, gxw, gxb, lam, wout, gn, rT)


def _route_kernel(lg_ref, idx_ref, gate_ref, cnt_ref, carry, *, tr):
    @pl.when(pl.program_id(0) == 0)
    def _():
        carry[...] = jnp.zeros(carry.shape, F32)

    lg = lg_ref[...]
    E = lg.shape[0]
    eidx = lax.broadcasted_iota(jnp.int32, lg.shape, 0).astype(F32)
    m1 = jnp.max(lg, axis=0, keepdims=True)
    i1 = jnp.min(jnp.where(lg == m1, eidx, float(E)), axis=0, keepdims=True)
    oh1 = eidx == i1
    lg2 = jnp.where(oh1, -jnp.inf, lg)
    m2 = jnp.max(lg2, axis=0, keepdims=True)
    i2 = jnp.min(jnp.where(lg2 == m2, eidx, float(E)), axis=0, keepdims=True)
    oh2 = eidx == i2
    ex = jnp.exp(m2 - m1)
    g1 = 1.0 / (1.0 + ex)
    g2 = ex / (1.0 + ex)

    both = jnp.where(oh1 | oh2, 1.0, 0.0).astype(BF16)
    ti = lax.broadcasted_iota(jnp.int32, (tr, tr), 0)
    tj = lax.broadcasted_iota(jnp.int32, (tr, tr), 1)
    before = jnp.where(ti < tj, 1.0, 0.0).astype(BF16)
    rank = jnp.dot(both, before, preferred_element_type=F32) + carry[...][:, 0:1]
    r1 = jnp.sum(jnp.where(oh1, rank, 0.0), axis=0, keepdims=True)
    r2 = jnp.sum(jnp.where(oh2, rank, 0.0), axis=0, keepdims=True)
    tot = carry[...] + jnp.sum(both.astype(F32), axis=1, keepdims=True)
    carry[...] = tot
    cnt_ref[...] = tot.astype(jnp.int32)

    orow = lax.broadcasted_iota(jnp.int32, (8, tr), 0)
    packed = jnp.where(orow == 0, i1, jnp.where(orow == 1, i2, jnp.where(orow == 2, r1, r2)))
    idx_ref[...] = packed.astype(jnp.int32)
    gate_ref[...] = jnp.where(orow == 0, g1, g2)


def _route(lgT, tr):
    E, T = lgT.shape
    kern = functools.partial(_route_kernel, tr=tr)
    return pl.pallas_call(
        kern,
        grid=(T // tr,),
        in_specs=[pl.BlockSpec((E, tr), lambda i: (0, i))],
        out_specs=[pl.BlockSpec((8, tr), lambda i: (0, i)),
                   pl.BlockSpec((8, tr), lambda i: (0, i)),
                   pl.BlockSpec((E, 128), lambda i: (0, 0))],
        out_shape=[jax.ShapeDtypeStruct((8, T), jnp.int32),
                   jax.ShapeDtypeStruct((8, T), F32),
                   jax.ShapeDtypeStruct((E, 128), jnp.int32)],
        scratch_shapes=[pltpu.VMEM((E, 128), F32)],
        compiler_params=pltpu.CompilerParams(dimension_semantics=("arbitrary",)),
        name="route",
    )(lgT)


def _dispatch_kernel(zt_ref, pos_ref, x_ref, xs_out, zero_scr, sem, zsem, *, td, rm):

    @pl.when(pl.program_id(0) == 0)
    def _():
        zero_scr[...] = jnp.zeros(zero_scr.shape, F32)
        for i in range(zt_ref.shape[0]):
            dst = xs_out.at[pl.ds(pl.multiple_of(zt_ref[i] * rm, rm), rm)]
            cp = pltpu.make_async_copy(zero_scr, dst, zsem)
            cp.start()
            cp.wait()

    def row_copy(g, u, k, dst_row):
        return pltpu.make_async_copy(x_ref.at[g, pl.ds(u, 1)], xs_out.at[pl.ds(dst_row, 1)], sem)

    def issue(g, c):
        for u in range(SUBLANES):
            for k in range(2):
                row_copy(g, u, k, pos_ref[g, 2 * u + k]).start(priority=k)
        return c

    lax.fori_loop(0, td // SUBLANES, issue, 0)

    def drain(g, c):
        for u in range(SUBLANES):
            for k in range(2):
                row_copy(0, 0, k, 0).wait()
        return c

    lax.fori_loop(0, td // SUBLANES, drain, 0)


def _dispatch(zero_tiles, pos_g, x, n_rows, td, rm):
    T, D = x.shape
    kern = functools.partial(_dispatch_kernel, td=td, rm=rm)
    gd = td // SUBLANES
    return pl.pallas_call(
        kern,
        grid_spec=pltpu.PrefetchScalarGridSpec(
            num_scalar_prefetch=1,
            grid=(T // td,),
            in_specs=[
                pl.BlockSpec((gd, 2 * SUBLANES), lambda i, zt: (i, 0), memory_space=pltpu.SMEM),
                pl.BlockSpec((gd, SUBLANES, D), lambda i, zt: (i, 0, 0)),
            ],
            out_specs=pl.BlockSpec(memory_space=pl.ANY),
            scratch_shapes=[pltpu.VMEM((rm, D), F32), pltpu.SemaphoreType.DMA,
                            pltpu.SemaphoreType.DMA],
        ),
        out_shape=jax.ShapeDtypeStruct((n_rows, D), F32),
        compiler_params=pltpu.CompilerParams(
            dimension_semantics=("arbitrary",), has_side_effects=True),
        name="dispatch",
    )(zero_tiles, pos_g, x.reshape(T // SUBLANES, SUBLANES, D))


def _expert_kernel(te_ref, nu_ref, xs_ref, wg_ref, wu_ref, wd_ref, ys_ref, acc_scr, *, nf):
    i = pl.program_id(0)
    j = pl.program_id(1)

    @pl.when(i < nu_ref[0])
    def _():
        @pl.when(j == 0)
        def _():
            acc_scr[...] = jnp.zeros(acc_scr.shape, F32)

        acc_scr[...] = _swiglu_acc(xs_ref[...].astype(BF16), wg_ref.at[0], wu_ref.at[0],
                                   wd_ref.at[0], acc_scr[...])

        @pl.when(j == nf - 1)
        def _():
            ys_ref[...] = acc_scr[...]

    @pl.when((i >= nu_ref[0]) & (j == nf - 1))
    def _():
        ys_ref[...] = jnp.zeros(ys_ref.shape, F32)


def _experts(tile_expert, n_used, xs, wg, wu, wd, tm, nf):
    NP, D = xs.shape
    E, _, Fd = wg.shape
    fc = Fd // nf
    nt = NP // tm

    def eff(i, nu):
        return jnp.minimum(i, nu[0] - 1)

    def chunk(i, j, nu):
        ie = eff(i, nu)
        je = jnp.where(i < nu[0], j, nf - 1)
        return jnp.where(ie % 2 == 0, je, nf - 1 - je)

    kern = functools.partial(_expert_kernel, nf=nf)
    return pl.pallas_call(
        kern,
        grid_spec=pltpu.PrefetchScalarGridSpec(
            num_scalar_prefetch=2,
            grid=(nt, nf),
            in_specs=[
                pl.BlockSpec((tm, D), lambda i, j, te, nu: (eff(i, nu), 0)),
                pl.BlockSpec((1, D, fc), lambda i, j, te, nu: (te[eff(i, nu)], 0, chunk(i, j, nu))),
                pl.BlockSpec((1, D, fc), lambda i, j, te, nu: (te[eff(i, nu)], 0, chunk(i, j, nu))),
                pl.BlockSpec((1, fc, D), lambda i, j, te, nu: (te[eff(i, nu)], chunk(i, j, nu), 0)),
            ],
            out_specs=pl.BlockSpec((tm, D), lambda i, j, te, nu: (i, 0)),
            scratch_shapes=[pltpu.VMEM((tm, D), F32)],
        ),
        out_shape=jax.ShapeDtypeStruct((NP, D), F32),
        compiler_params=pltpu.CompilerParams(
            dimension_semantics=("arbitrary", "arbitrary"),
            vmem_limit_bytes=_vmem_limit(56 * 1024 * 1024)),
        name="experts",
    )(tile_expert, n_used, xs, wg, wu, wd)


def _combine_kernel(pos_ref, gate_ref, x_ref, gn_ref, ys_hbm, out_ref, buf, sem, *, tc):
    def row_copy(g, u, k, src_row):
        return pltpu.make_async_copy(ys_hbm.at[pl.ds(src_row, 1)], buf.at[k, g, pl.ds(u, 1)], sem)

    def issue(g, c):
        for u in range(SUBLANES):
            for k in range(2):
                row_copy(g, u, k, pos_ref[g, 2 * u + k]).start(priority=k)
        return c

    lax.fori_loop(0, tc // SUBLANES, issue, 0)

    def drain(g, c):
        for u in range(SUBLANES):
            for k in range(2):
                row_copy(0, 0, k, 0).wait()
        return c

    lax.fori_loop(0, tc // SUBLANES, drain, 0)

    g = gate_ref[...]
    D = x_ref.shape[-1]
    x4 = (x_ref[...] + g[:, 0:1] * buf[0].reshape(tc, D) + g[:, 1:2] * buf[1].reshape(tc, D))
    out_ref[...] = _rms(x4, gn_ref[...], NORM_EPS)


def _combine(pos_g, gates_col, x, gn, ys, tc):
    T, D = x.shape
    kern = functools.partial(_combine_kernel, tc=tc)
    gc = tc // SUBLANES
    return pl.pallas_call(
        kern,
        grid=(T // tc,),
        in_specs=[
            pl.BlockSpec((gc, 2 * SUBLANES), lambda i: (i, 0), memory_space=pltpu.SMEM),
            pl.BlockSpec((tc, 2), lambda i: (i, 0)),
            pl.BlockSpec((tc, D), lambda i: (i, 0)),
            _const_spec((1, D)),
            pl.BlockSpec(memory_space=pl.ANY),
        ],
        out_specs=pl.BlockSpec((tc, D), lambda i: (i, 0)),
        out_shape=jax.ShapeDtypeStruct((T, D), F32),
        scratch_shapes=[pltpu.VMEM((2, gc, SUBLANES, D), F32), pltpu.SemaphoreType.DMA],
        compiler_params=pltpu.CompilerParams(
            dimension_semantics=("arbitrary",),
            vmem_limit_bytes=_vmem_limit(40 * 1024 * 1024)),
        name="combine",
    )(pos_g, gates_col, x, gn, ys)


def _tile(n, pref):
    t = min(n, pref)
    assert n % t == 0, (n, t)
    return t


def kernel(x, norm_mix, norm_ffn, norm_final, rel_bias, attn_w_qkv, attn_w_o, attn_lambda_q1, attn_lambda_k1, attn_lambda_q2, attn_lambda_k2, attn_subln, lru_w_in, lru_conv_w, lru_conv_b, lru_gate_a_w, lru_gate_a_b, lru_gate_x_w, lru_gate_x_b, lru_lambda, lru_w_out, ffn_w_gate, ffn_w_up, ffn_w_down, moe_router, moe_w_gate, moe_w_up, moe_w_down):
    B, S, D = x.shape
    T = B * S
    assert D == N_HEADS * V_DIM
    row = lambda v: v.reshape(1, -1).astype(F32)

    ta = _tile(S, ATTN_TILE)
    wqkv = attn_w_qkv[0]
    wqT = (wqkv[:, :D] * (SCALE * LOG2E)).T.astype(BF16)
    wk = wqkv[:, D:2 * D].astype(BF16)
    wvT = wqkv[:, 2 * D:].T.astype(BF16)
    qT, k, vT = _qkv(x, row(norm_mix[0]), wqT, wk, wvT, ta)
    lqk = jnp.stack([attn_lambda_q1[0], attn_lambda_k1[0],
                     attn_lambda_q2[0], attn_lambda_k2[0]]).astype(F32)
    bias = _bias_tiles(rel_bias, ta)
    o = _attention(lqk, qT, k, vT, bias, attn_subln[0].reshape(V_DIM, 1).astype(F32), ta)

    tm = _tile(T, TOKEN_TILE)
    x2, h2 = _ffn(o.reshape(T, D), x.reshape(T, D), attn_w_o[0].astype(BF16), row(norm_ffn[0]),
                  ffn_w_gate[0].astype(BF16), ffn_w_up[0].astype(BF16),
                  ffn_w_down[0].astype(BF16), row(norm_mix[1]), tm)

    ts = _tile(S, LRU_TILE)
    x3, h3, lg = _lru(h2.reshape(B, S, D), x2.reshape(B, S, D), lru_w_in[0].astype(BF16),
                      lru_conv_w[0].astype(F32), row(lru_conv_b[0]),
                      lru_gate_a_w[0].astype(BF16), row(lru_gate_a_b[0]),
                      lru_gate_x_w[0].astype(BF16), row(lru_gate_x_b[0]),
                      row(lru_lambda[0]), lru_w_out[0].astype(BF16), row(norm_ffn[1]),
                      moe_router[0].T.astype(F32), ts)
    x3 = x3.reshape(T, D)
    h3 = h3.reshape(T, D)
    E = moe_router.shape[-1]
    lgT = jnp.transpose(lg, (1, 0, 2)).reshape(E, T)

    idx, gates, counts = _route(lgT, _tile(T, ROUTE_TILE))
    rm = EXPERT_ROW_TILE
    cnt = counts[:, 0]
    padded = ((cnt + rm - 1) // rm) * rm
    ends = jnp.cumsum(padded)
    starts = ends - padded
    pos = jnp.stack([starts[idx[0]] + idx[2], starts[idx[1]] + idx[3]])
    n_tiles = (2 * T) // rm + E
    tile_rows = jnp.arange(n_tiles, dtype=jnp.int32) * rm
    tile_expert = jnp.minimum(
        jnp.sum((tile_rows[:, None] >= ends[None, :]).astype(jnp.int32), axis=1), E - 1)
    n_used = (ends[-1] // rm).astype(jnp.int32).reshape(1)

    last_tile = jnp.maximum(ends // rm - 1, 0)
    tail_tile = jnp.minimum(n_used[0] + jnp.arange(E, dtype=jnp.int32), n_tiles - 1)
    zero_tiles = jnp.concatenate([last_tile, tail_tile]).astype(jnp.int32)

    td = _tile(T, DISPATCH_TILE)
    pos_g = pos.T.reshape(T // SUBLANES, 2 * SUBLANES)
    xs = _dispatch(zero_tiles, pos_g, h3, n_tiles * rm, td, rm)
    ys = _experts(tile_expert, n_used, xs, moe_w_gate[0].astype(BF16),
                  moe_w_up[0].astype(BF16), moe_w_down[0].astype(BF16), rm, EXPERT_FF_CHUNKS)
    out = _combine(pos_g, gates[:2].T, x3, row(norm_final), ys, td)
    return out.reshape(B, S, D)
```

```python
import functools
import math

import numpy as np
import jax
import jax.numpy as jnp
from jax import lax
from jax.experimental import pallas as pl
from jax.experimental.pallas import tpu as pltpu

F32 = jnp.float32
BF16 = jnp.bfloat16

N_HEADS = 8
HEAD_DIM = 64
V_DIM = 2 * HEAD_DIM
SCALE = HEAD_DIM ** -0.5
LOG2E = math.log2(math.e)
NEG_INF = -1e30
NUM_BUCKETS = 32
MAX_EXACT = NUM_BUCKETS // 2
MAX_DISTANCE = 128
N_LRU_BLOCKS = 10
LRU_BLOCK = 128
CONV_WIDTH = 4
LRU_C = 8.0
N_EXPERTS = 8
NORM_EPS = 1e-6
SUBLN_EPS = 1e-5
LAM_INIT_0 = 0.8 - 0.6 * math.exp(-0.3 * 0)

V7X_VMEM_BYTES = 64 * 1024 * 1024
SUBLANES = 8
BF16_ROWS = 16

ATTN_TILE = 512
ATTN_UNROLL = 4
TOKEN_TILE = 512
LRU_TILE = 256
ROUTE_TILE = 1024
EXPERT_ROW_TILE = 512
EXPERT_FF_CHUNKS = 2
FF_SLAB = 256
PROJ_SLAB = 256
DISPATCH_TILE = 512

NT_DIMS = (((1,), (1,)), ((), ()))


def _vmem_limit(nbytes):
    return int(min(nbytes, V7X_VMEM_BYTES - 6 * 1024 * 1024))


def _rms(x, g, eps):
    ms = jnp.mean(x * x, axis=-1, keepdims=True)
    return x * lax.rsqrt(ms + eps) * g


def _const_spec(shape):
    nd = len(shape)
    return pl.BlockSpec(shape, lambda *_: (0,) * nd, pipeline_mode=pl.Buffered(1))


def _qkv_kernel(x_ref, g_ref, wqT_ref, wk_ref, wvT_ref, qT_ref, k_ref, vT_ref):
    hn = _rms(x_ref[0], g_ref[...], NORM_EPS).astype(BF16)
    qT_ref[0, 0] = lax.dot_general(wqT_ref[...], hn, NT_DIMS,
                                   preferred_element_type=F32).astype(BF16)
    k_ref[0] = jnp.dot(hn, wk_ref[...], preferred_element_type=F32).astype(BF16)
    vT_ref[0, 0] = lax.dot_general(wvT_ref[...], hn, NT_DIMS,
                                   preferred_element_type=F32).astype(BF16)


def _qkv(x, g, wqT, wk, wvT, tm):
    B, S, D = x.shape
    n = S // tm
    return pl.pallas_call(
        _qkv_kernel,
        grid=(B, n),
        in_specs=[
            pl.BlockSpec((1, tm, D), lambda b, i: (b, i, 0)),
            _const_spec((1, D)),
            _const_spec((D, D)),
            _const_spec((D, D)),
            _const_spec((D, D)),
        ],
        out_specs=[
            pl.BlockSpec((1, 1, D, tm), lambda b, i: (b, i, 0, 0)),
            pl.BlockSpec((1, tm, D), lambda b, i: (b, i, 0)),
            pl.BlockSpec((1, 1, D, tm), lambda b, i: (b, i, 0, 0)),
        ],
        out_shape=[
            jax.ShapeDtypeStruct((B, n, D, tm), BF16),
            jax.ShapeDtypeStruct((B, S, D), BF16),
            jax.ShapeDtypeStruct((B, n, D, tm), BF16),
        ],
        compiler_params=pltpu.CompilerParams(
            dimension_semantics=("arbitrary", "arbitrary"),
            vmem_limit_bytes=_vmem_limit(40 * 1024 * 1024)),
        name="qkv",
    )(x, g, wqT, wk, wvT)


def _t5_bucket_np(n):
    nf = np.maximum(n, 1).astype(np.float32)
    large = MAX_EXACT + (np.log(nf / np.float32(MAX_EXACT))
                         / np.float32(math.log(MAX_DISTANCE / MAX_EXACT))
                         * np.float32(NUM_BUCKETS - MAX_EXACT)).astype(np.int32)
    large = np.minimum(large, NUM_BUCKETS - 1)
    return np.where(n < MAX_EXACT, n, large)


def _bias_kernel(w_ref, out_ref, *, t):
    full = jnp.broadcast_to(w_ref[0], (t, 4 * t))
    c = pltpu.roll(full, 0, axis=1, stride=1, stride_axis=0)
    out_ref[0, 0] = c[:, :t]
    out_ref[0, 1] = c[:, t:2 * t]


def _bias_tiles(rel_bias, t):
    assert t >= MAX_DISTANCE and _t5_bucket_np(np.array([t + 1]))[0] == NUM_BUCKETS - 1
    H = rel_bias.shape[1]
    rb = rel_bias.astype(F32) - rel_bias[NUM_BUCKETS - 1].astype(F32)[None, :]
    near = jnp.take(rb, jnp.asarray(_t5_bucket_np(np.arange(2 * t))), axis=0).T
    w = jnp.concatenate([near, jnp.full((H, 2 * t), NEG_INF, F32)], axis=1) * LOG2E
    return pl.pallas_call(
        functools.partial(_bias_kernel, t=t),
        grid=(H,),
        in_specs=[pl.BlockSpec((1, 1, 4 * t), lambda h: (h, 0, 0))],
        out_specs=pl.BlockSpec((1, 2, t, t), lambda h: (h, 0, 0, 0)),
        out_shape=jax.ShapeDtypeStruct((H, 2, t, t), F32),
        compiler_params=pltpu.CompilerParams(dimension_semantics=("arbitrary",)),
        name="bias",
    )(w.reshape(H, 1, 4 * t))


def _attn_kernel(lqk_ref, qT_ref, k_ref, vT_ref, bias_ref, g_ref, o_ref,
                 q_scr, sa_scr, sb_scr, sc_scr, mba_scr, mbb_scr, mbc_scr, m_scr, acc_scr,
                 *, t, n):
    row = lax.broadcasted_iota(jnp.int32, (V_DIM, t), 0)
    for i in range(n):
        qT = qT_ref[0, i]
        zero = jnp.zeros_like(qT)
        q_scr[i, 0] = jnp.where(row < HEAD_DIM, qT, zero)
        q_scr[i, 1] = jnp.where(row >= HEAD_DIM, qT, zero)

    lqk = lqk_ref[...]
    e1 = jnp.exp(jnp.sum(lqk[0:1] * lqk[1:2], axis=-1, keepdims=True))
    e2 = jnp.exp(jnp.sum(lqk[2:3] * lqk[3:4], axis=-1, keepdims=True))
    lam = e1 - e2 + LAM_INIT_0

    def reset():
        m_scr[...] = jnp.full(m_scr.shape, NEG_INF, F32)
        acc_scr[...] = jnp.zeros(acc_scr.shape, F32)

    def scores(qi, j, bias_idx, s_scr, mb_scr):
        kc = k_ref[0, pl.ds(pl.multiple_of(j * t, t), t), :]
        for c in range(2):
            s = jnp.dot(kc, q_scr[qi, c], preferred_element_type=F32)
            if bias_idx is not None:
                s = s + bias_ref[0, bias_idx]
            s_scr[c] = s
            mb_scr[c] = jnp.max(s, axis=0, keepdims=True)

    ones_rows = jnp.where(lax.broadcasted_iota(jnp.int32, (BF16_ROWS, t), 0) == 0, 1.0, 0.0).astype(BF16)

    def consume(j, s_scr, mb_scr):
        vt = jnp.concatenate([vT_ref[0, j], ones_rows], axis=0)
        for c in range(2):
            m_old = m_scr[c]
            m_new = jnp.maximum(m_old, mb_scr[c])
            alpha = jnp.exp2(m_old - m_new)
            m_scr[c] = m_new
            p = jnp.exp2(s_scr[c] - m_new).astype(BF16)
            acc_scr[c] = alpha * acc_scr[c] + jnp.dot(vt, p, preferred_element_type=F32)

    def finalize(qi):
        l0 = acc_scr[0, V_DIM:V_DIM + 1, :]
        l1 = acc_scr[1, V_DIM:V_DIM + 1, :]
        o = acc_scr[0, :V_DIM, :] / l0 - lam * (acc_scr[1, :V_DIM, :] / l1)
        ms = jnp.mean(o * o, axis=0, keepdims=True)
        on = o * lax.rsqrt(ms + SUBLN_EPS) * g_ref[...] * (1.0 - LAM_INIT_0)
        o_ref[0, pl.ds(pl.multiple_of(qi * t, t), t), :] = on.T.astype(BF16)

    a_bufs = (sa_scr, mba_scr)
    b_bufs = (sb_scr, mbb_scr)
    c_bufs = (sc_scr, mbc_scr)

    reset()
    scores(0, 0, 0, *c_bufs)
    consume(0, *c_bufs)
    finalize(0)
    if n == 1:
        return
    scores(1, 1, 0, *c_bufs)

    def tile(qi, carry):
        reset()
        scores(qi, qi - 1, 1, *a_bufs)
        consume(qi, *c_bufs)
        n_far = qi - 1

        def far_steps(first, count):
            for s in range(count):
                into, held = (b_bufs, a_bufs) if s % 2 == 0 else (a_bufs, b_bufs)
                scores(qi, first - 1 - s, None, *into)
                consume(first - s, *held)

        def trip(pp, c2):
            far_steps(n_far - ATTN_UNROLL * pp, ATTN_UNROLL)
            return c2

        lax.fori_loop(0, n_far // ATTN_UNROLL, trip, 0)
        nxt = jnp.minimum(qi + 1, n - 1)
        rem = n_far % ATTN_UNROLL

        for r in range(ATTN_UNROLL):
            @pl.when(rem == r)
            def _(r=r):
                far_steps(r, r)
                scores(nxt, nxt, 0, *c_bufs)
                consume(0, *(a_bufs if r % 2 == 0 else b_bufs))

        finalize(qi)
        return carry

    lax.fori_loop(1, n, tile, 0)


def _attention(lqk, qT, k, vT, bias, g_col, t):
    B, n, D, _ = qT.shape
    S = n * t
    H = D // V_DIM
    kern = functools.partial(_attn_kernel, t=t, n=n)
    return pl.pallas_call(
        kern,
        grid=(B, H),
        in_specs=[
            _const_spec(lqk.shape),
            pl.BlockSpec((1, n, V_DIM, t), lambda b, h: (b, 0, h, 0)),
            pl.BlockSpec((1, S, V_DIM), lambda b, h: (b, 0, h)),
            pl.BlockSpec((1, n, V_DIM, t), lambda b, h: (b, 0, h, 0)),
            pl.BlockSpec((1, 2, t, t), lambda b, h: (h, 0, 0, 0)),
            _const_spec((V_DIM, 1)),
        ],
        out_specs=pl.BlockSpec((1, S, V_DIM), lambda b, h: (b, 0, h)),
        out_shape=jax.ShapeDtypeStruct((B, S, D), BF16),
        scratch_shapes=[
            pltpu.VMEM((n, 2, V_DIM, t), BF16),
            pltpu.VMEM((2, t, t), F32),
            pltpu.VMEM((2, t, t), F32),
            pltpu.VMEM((2, t, t), F32),
            pltpu.VMEM((2, 1, t), F32),
            pltpu.VMEM((2, 1, t), F32),
            pltpu.VMEM((2, 1, t), F32),
            pltpu.VMEM((2, 1, t), F32),
            pltpu.VMEM((2, V_DIM + BF16_ROWS, t), F32),
        ],
        compiler_params=pltpu.CompilerParams(
            dimension_semantics=("arbitrary", "arbitrary"),
            vmem_limit_bytes=_vmem_limit(52 * 1024 * 1024)),
        name="attn",
    )(lqk, qT, k, vT, bias, g_col)


def _swiglu_acc(h, wg, wu, wd, acc):
    width = wg.shape[-1]
    step = FF_SLAB if width % FF_SLAB == 0 else width
    for c in range(width // step):
        sl = slice(c * step, (c + 1) * step)
        gt = jnp.dot(h, wg[:, sl], preferred_element_type=F32)
        up = jnp.dot(h, wu[:, sl], preferred_element_type=F32)
        a = (gt * jax.nn.sigmoid(gt) * up).astype(BF16)
        acc = acc + jnp.dot(a, wd[sl, :], preferred_element_type=F32)
    return acc


def _ffn_kernel(o_ref, x_ref, wo_ref, g1_ref, wg_ref, wu_ref, wd_ref, g2_ref, x2_ref, h2_ref):
    x1 = x_ref[...] + jnp.dot(o_ref[...], wo_ref[...], preferred_element_type=F32)
    h1 = _rms(x1, g1_ref[...], NORM_EPS).astype(BF16)
    x2 = _swiglu_acc(h1, wg_ref, wu_ref, wd_ref, x1)
    x2_ref[...] = x2
    h2_ref[...] = _rms(x2, g2_ref[...], NORM_EPS).astype(BF16)


def _ffn(o, x, wo, g1, wg, wu, wd, g2, tm):
    T, D = x.shape
    Fd = wg.shape[1]
    kern = _ffn_kernel
    tile = lambda i: (i, 0)
    return pl.pallas_call(
        kern,
        grid=(T // tm,),
        in_specs=[
            pl.BlockSpec((tm, D), tile),
            pl.BlockSpec((tm, D), tile),
            _const_spec((D, D)),
            _const_spec((1, D)),
            _const_spec((D, Fd)),
            _const_spec((D, Fd)),
            _const_spec((Fd, D)),
            _const_spec((1, D)),
        ],
        out_specs=[pl.BlockSpec((tm, D), tile), pl.BlockSpec((tm, D), tile)],
        out_shape=[jax.ShapeDtypeStruct((T, D), F32), jax.ShapeDtypeStruct((T, D), BF16)],
        compiler_params=pltpu.CompilerParams(
            dimension_semantics=("arbitrary",),
            vmem_limit_bytes=_vmem_limit(56 * 1024 * 1024)),
        name="ffn",
    )(o, x, wo, g1, wg, wu, wd, g2)


def _gelu_tanh(x):
    return 0.5 * x * (1.0 + jnp.tanh(math.sqrt(2.0 / math.pi) * (x + 0.044715 * (x * x * x))))


def _lru_kernel(h2_ref, h2n_ref, x2_ref, win_ref, cw_ref, cb_ref, gaw_ref, gab_ref, gxw_ref,
                gxb_ref, lam_ref, wout_ref, gn_ref, rT_ref, x3_ref, h3_ref, lg_ref,
                gx0_scr, gx1_scr, xbuf, a_scr, b_scr, h_scr, hcar, *, ts, dr):
    si = pl.program_id(1)
    halo = SUBLANES

    @pl.when(si == 0)
    def _():
        xbuf[0:halo, :] = jnp.zeros((halo, dr), F32)
        hcar[...] = jnp.zeros(hcar.shape, F32)

    @pl.when((pl.program_id(0) == 0) & (si == 0))
    def _():
        gx0_scr[...] = jnp.dot(h2_ref[0, 0:ts, :], win_ref[...], preferred_element_type=F32)

    z = -lam_ref[...]
    ez = jnp.exp(-jnp.abs(z))
    u = 1.0 + ez
    du = u - 1.0
    log1p_ez = jnp.where(du == 0.0, ez, jnp.log(u) * (ez / du))
    softplus = jnp.maximum(z, 0.0) + log1p_ez

    def in_proj_slabs(load_rows, gx_scr):
        def slab(c0):
            def emit():
                gx_scr[:, c0:c0 + PROJ_SLAB] = jnp.dot(
                    load_rows(), win_ref[:, c0:c0 + PROJ_SLAB], preferred_element_type=F32)
            return emit
        return [slab(c0) for c0 in range(0, 2 * dr, PROJ_SLAB)]

    _lru_tile(0, gx0_scr, in_proj_slabs(lambda: h2_ref[0, ts:2 * ts, :], gx1_scr), softplus,
              x2_ref, cw_ref, cb_ref, gaw_ref, gab_ref, gxw_ref, gxb_ref,
              wout_ref, gn_ref, rT_ref, x3_ref, h3_ref, lg_ref, xbuf, a_scr, b_scr, h_scr, hcar,
              ts=ts, dr=dr)
    _lru_tile(ts, gx1_scr, in_proj_slabs(lambda: h2n_ref[0], gx0_scr), softplus,
              x2_ref, cw_ref, cb_ref, gaw_ref, gab_ref, gxw_ref, gxb_ref,
              wout_ref, gn_ref, rT_ref, x3_ref, h3_ref, lg_ref, xbuf, a_scr, b_scr, h_scr, hcar,
              ts=ts, dr=dr)


def _lru_tile(r0, gx_scr, background, softplus, x2_ref, cw_ref, cb_ref, gaw_ref, gab_ref, gxw_ref,
              gxb_ref, wout_ref, gn_ref, rT_ref, x3_ref, h3_ref, lg_ref, xbuf, a_scr, b_scr, h_scr,
              hcar, *, ts, dr):
    halo = SUBLANES
    pending = list(background)
    per_phase = -(-len(pending) // 10)

    def emit_background():
        for _ in range(min(per_phase, len(pending))):
            pending.pop(0)()

    emit_background()
    gate = _gelu_tanh(gx_scr[:, :dr])
    xb = gx_scr[:, dr:]
    xbuf[halo:halo + ts, :] = xb

    emit_background()
    cw = cw_ref[...]
    xc = cb_ref[...] + cw[CONV_WIDTH - 1:CONV_WIDTH] * xb
    for j in range(CONV_WIDTH - 1):
        back = CONV_WIDTH - 1 - j
        xc = xc + cw[j:j + 1] * xbuf[halo - back:halo - back + ts, :]
    xbuf[0:halo, :] = xb[ts - halo:ts, :]

    emit_background()
    xcb = xc.astype(BF16)
    r_parts, i_parts = [], []
    for n in range(dr // LRU_BLOCK):
        blk = xcb[:, n * LRU_BLOCK:(n + 1) * LRU_BLOCK]
        r_parts.append(jnp.dot(blk, gaw_ref[n], preferred_element_type=F32))
        i_parts.append(jnp.dot(blk, gxw_ref[n], preferred_element_type=F32))
    emit_background()
    r = jax.nn.sigmoid(jnp.concatenate(r_parts, axis=1) + gab_ref[...])
    emit_background()
    ig = jax.nn.sigmoid(jnp.concatenate(i_parts, axis=1) + gxb_ref[...])
    emit_background()
    log_a = (-LRU_C) * r * softplus
    a = jnp.exp(log_a)
    b = jnp.sqrt(1.0 - a * a) * (ig * xc)

    emit_background()
    ng = ts // SUBLANES
    a3 = a.reshape(ng, SUBLANES, dr)
    b3 = b.reshape(ng, SUBLANES, dr)
    sub = lax.broadcasted_iota(jnp.int32, a3.shape, 1)
    d = 1
    while d < SUBLANES:
        keep = sub >= d
        a_s = pltpu.roll(a3, d, axis=1)
        b_s = pltpu.roll(b3, d, axis=1)
        b3 = jnp.where(keep, a3 * b_s + b3, b3)
        a3 = jnp.where(keep, a3 * a_s, a3)
        d *= 2
        emit_background()
    a_scr[...] = a3.reshape(ts, dr)
    b_scr[...] = b3.reshape(ts, dr)

    emit_background()
    assert not pending
    hprev = hcar[...]
    for g in range(ng):
        rows = slice(g * SUBLANES, (g + 1) * SUBLANES)
        hg = a_scr[rows, :] * hprev + b_scr[rows, :]
        h_scr[rows, :] = hg
        hprev = jnp.broadcast_to(hg[SUBLANES - 1:SUBLANES, :], (SUBLANES, dr))
    hcar[...] = hprev

    y = (h_scr[...] * gate).astype(BF16)
    x3 = x2_ref[0, r0:r0 + ts, :] + jnp.dot(y, wout_ref[...], preferred_element_type=F32)
    x3_ref[0, r0:r0 + ts, :] = x3

    h3 = _rms(x3, gn_ref[...], NORM_EPS)
    h3_ref[0, r0:r0 + ts, :] = h3
    h_hi = h3.astype(BF16)
    h_lo = (h3 - h_hi.astype(F32)).astype(BF16)
    rT = rT_ref[...]
    r_hi = rT.astype(BF16)
    r_lo = (rT - r_hi.astype(F32)).astype(BF16)
    lg = lax.dot_general(r_hi, h_hi, NT_DIMS, preferred_element_type=F32)
    lg = lg + lax.dot_general(r_lo, h_hi, NT_DIMS, preferred_element_type=F32)
    lg = lg + lax.dot_general(r_hi, h_lo, NT_DIMS, preferred_element_type=F32)
    lg_ref[0, :, r0:r0 + ts] = lg


def _lru(h2, x2, win, cw, cb, gaw, gab, gxw, gxb, lam, wout, gn, rT, ts):
    B, S, D = x2.shape
    dr = wout.shape[0]
    E = rT.shape[0]
    n = S // (2 * ts)
    kern = functools.partial(_lru_kernel, ts=ts, dr=dr)
    tile = lambda b, i: (b, i, 0)

    def next_first(b, i):
        last = i + 1 >= n
        return (jnp.where(last, jnp.minimum(b + 1, B - 1), b), jnp.where(last, 0, 2 * (i + 1)), 0)

    return pl.pallas_call(
        kern,
        grid=(B, n),
        in_specs=[
            pl.BlockSpec((1, 2 * ts, D), tile),
            pl.BlockSpec((1, ts, D), next_first),
            pl.BlockSpec((1, 2 * ts, D), tile),
            _const_spec(win.shape),
            _const_spec(cw.shape),
            _const_spec(cb.shape),
            _const_spec(gaw.shape),
            _const_spec(gab.shape),
            _const_spec(gxw.shape),
            _const_spec(gxb.shape),
            _const_spec(lam.shape),
            _const_spec(wout.shape),
            _const_spec(gn.shape),
            _const_spec(rT.shape),
        ],
        out_specs=[pl.BlockSpec((1, 2 * ts, D), tile),
                   pl.BlockSpec((1, 2 * ts, D), tile),
                   pl.BlockSpec((1, E, 2 * ts), lambda b, i: (b, 0, i))],
        out_shape=[jax.ShapeDtypeStruct((B, S, D), F32),
                   jax.ShapeDtypeStruct((B, S, D), F32),
                   jax.ShapeDtypeStruct((B, E, S), F32)],
        scratch_shapes=[
            pltpu.VMEM((ts, 2 * dr), F32),
            pltpu.VMEM((ts, 2 * dr), F32),
            pltpu.VMEM((SUBLANES + ts, dr), F32),
            pltpu.VMEM((ts, dr), F32),
            pltpu.VMEM((ts, dr), F32),
            pltpu.VMEM((ts, dr), F32),
            pltpu.VMEM((SUBLANES, dr), F32),
        ],
        compiler_params=pltpu.CompilerParams(
            dimension_semantics=("arbitrary", "arbitrary"),
            vmem_limit_bytes=_vmem_limit(48 * 1024 * 1024)),
        name="lru",
    )(h2, h2, x2, win, cw, cb, gaw, gab, gxw, gxb, lam, wout, gn, rT)


def _route_kernel(lg_ref, idx_ref, gate_ref, cnt_ref, carry, *, tr):
    @pl.when(pl.program_id(0) == 0)
    def _():
        carry[...] = jnp.zeros(carry.shape, F32)

    lg = lg_ref[...]
    E = lg.shape[0]
    eidx = lax.broadcasted_iota(jnp.int32, lg.shape, 0).astype(F32)
    m1 = jnp.max(lg, axis=0, keepdims=True)
    i1 = jnp.min(jnp.where(lg == m1, eidx, float(E)), axis=0, keepdims=True)
    oh1 = eidx == i1
    lg2 = jnp.where(oh1, -jnp.inf, lg)
    m2 = jnp.max(lg2, axis=0, keepdims=True)
    i2 = jnp.min(jnp.where(lg2 == m2, eidx, float(E)), axis=0, keepdims=True)
    oh2 = eidx == i2
    ex = jnp.exp(m2 - m1)
    g1 = 1.0 / (1.0 + ex)
    g2 = ex / (1.0 + ex)

    both = jnp.where(oh1 | oh2, 1.0, 0.0).astype(BF16)
    ti = lax.broadcasted_iota(jnp.int32, (tr, tr), 0)
    tj = lax.broadcasted_iota(jnp.int32, (tr, tr), 1)
    before = jnp.where(ti < tj, 1.0, 0.0).astype(BF16)
    rank = jnp.dot(both, before, preferred_element_type=F32) + carry[...][:, 0:1]
    r1 = jnp.sum(jnp.where(oh1, rank, 0.0), axis=0, keepdims=True)
    r2 = jnp.sum(jnp.where(oh2, rank, 0.0), axis=0, keepdims=True)
    tot = carry[...] + jnp.sum(both.astype(F32), axis=1, keepdims=True)
    carry[...] = tot
    cnt_ref[...] = tot.astype(jnp.int32)

    orow = lax.broadcasted_iota(jnp.int32, (8, tr), 0)
    packed = jnp.where(orow == 0, i1, jnp.where(orow == 1, i2, jnp.where(orow == 2, r1, r2)))
    idx_ref[...] = packed.astype(jnp.int32)
    gate_ref[...] = jnp.where(orow == 0, g1, g2)


def _route(lgT, tr):
    E, T = lgT.shape
    kern = functools.partial(_route_kernel, tr=tr)
    return pl.pallas_call(
        kern,
        grid=(T // tr,),
        in_specs=[pl.BlockSpec((E, tr), lambda i: (0, i))],
        out_specs=[pl.BlockSpec((8, tr), lambda i: (0, i)),
                   pl.BlockSpec((8, tr), lambda i: (0, i)),
                   pl.BlockSpec((E, 128), lambda i: (0, 0))],
        out_shape=[jax.ShapeDtypeStruct((8, T), jnp.int32),
                   jax.ShapeDtypeStruct((8, T), F32),
                   jax.ShapeDtypeStruct((E, 128), jnp.int32)],
        scratch_shapes=[pltpu.VMEM((E, 128), F32)],
        compiler_params=pltpu.CompilerParams(dimension_semantics=("arbitrary",)),
        name="route",
    )(lgT)


def _dispatch_kernel(zt_ref, pos_ref, x_ref, xs_out, zero_scr, sem, zsem, *, td, rm):

    @pl.when(pl.program_id(0) == 0)
    def _():
        zero_scr[...] = jnp.zeros(zero_scr.shape, F32)
        for i in range(zt_ref.shape[0]):
            dst = xs_out.at[pl.ds(pl.multiple_of(zt_ref[i] * rm, rm), rm)]
            cp = pltpu.make_async_copy(zero_scr, dst, zsem)
            cp.start()
            cp.wait()

    def row_copy(g, u, k, dst_row):
        return pltpu.make_async_copy(x_ref.at[g, pl.ds(u, 1)], xs_out.at[pl.ds(dst_row, 1)], sem)

    def issue(g, c):
        for u in range(SUBLANES):
            for k in range(2):
                row_copy(g, u, k, pos_ref[g, 2 * u + k]).start(priority=k)
        return c

    lax.fori_loop(0, td // SUBLANES, issue, 0)

    def drain(g, c):
        for u in range(SUBLANES):
            for k in range(2):
                row_copy(0, 0, k, 0).wait()
        return c

    lax.fori_loop(0, td // SUBLANES, drain, 0)


def _dispatch(zero_tiles, pos_g, x, n_rows, td, rm):
    T, D = x.shape
    kern = functools.partial(_dispatch_kernel, td=td, rm=rm)
    gd = td // SUBLANES
    return pl.pallas_call(
        kern,
        grid_spec=pltpu.PrefetchScalarGridSpec(
            num_scalar_prefetch=1,
            grid=(T // td,),
            in_specs=[
                pl.BlockSpec((gd, 2 * SUBLANES), lambda i, zt: (i, 0), memory_space=pltpu.SMEM),
                pl.BlockSpec((gd, SUBLANES, D), lambda i, zt: (i, 0, 0)),
            ],
            out_specs=pl.BlockSpec(memory_space=pl.ANY),
            scratch_shapes=[pltpu.VMEM((rm, D), F32), pltpu.SemaphoreType.DMA,
                            pltpu.SemaphoreType.DMA],
        ),
        out_shape=jax.ShapeDtypeStruct((n_rows, D), F32),
        compiler_params=pltpu.CompilerParams(
            dimension_semantics=("arbitrary",), has_side_effects=True),
        name="dispatch",
    )(zero_tiles, pos_g, x.reshape(T // SUBLANES, SUBLANES, D))


def _expert_kernel(te_ref, nu_ref, xs_ref, wg_ref, wu_ref, wd_ref, ys_ref, *, nf):
    i = pl.program_id(0)
    j = pl.program_id(1)

    @pl.when(j == 0)
    def _():
        ys_ref[...] = jnp.zeros(ys_ref.shape, F32)

    @pl.when(i < nu_ref[0])
    def _():
        ys_ref[...] = _swiglu_acc(xs_ref[...].astype(BF16), wg_ref.at[0], wu_ref.at[0],
                                  wd_ref.at[0], ys_ref[...])


def _experts(tile_expert, n_used, xs, wg, wu, wd, tm, nf):
    NP, D = xs.shape
    E, _, Fd = wg.shape
    fc = Fd // nf
    nt = NP // tm

    def eff(i, nu):
        return jnp.minimum(i, nu[0] - 1)

    def chunk(i, j, nu):
        ie = eff(i, nu)
        je = jnp.where(i < nu[0], j, nf - 1)
        return jnp.where(ie % 2 == 0, je, nf - 1 - je)

    kern = functools.partial(_expert_kernel, nf=nf)
    return pl.pallas_call(
        kern,
        grid_spec=pltpu.PrefetchScalarGridSpec(
            num_scalar_prefetch=2,
            grid=(nt, nf),
            in_specs=[
                pl.BlockSpec((tm, D), lambda i, j, te, nu: (eff(i, nu), 0)),
                pl.BlockSpec((1, D, fc), lambda i, j, te, nu: (te[eff(i, nu)], 0, chunk(i, j, nu))),
                pl.BlockSpec((1, D, fc), lambda i, j, te, nu: (te[eff(i, nu)], 0, chunk(i, j, nu))),
                pl.BlockSpec((1, fc, D), lambda i, j, te, nu: (te[eff(i, nu)], chunk(i, j, nu), 0)),
            ],
            out_specs=pl.BlockSpec((tm, D), lambda i, j, te, nu: (i, 0)),
        ),
        out_shape=jax.ShapeDtypeStruct((NP, D), F32),
        compiler_params=pltpu.CompilerParams(
            dimension_semantics=("arbitrary", "arbitrary"),
            vmem_limit_bytes=_vmem_limit(56 * 1024 * 1024)),
        name="experts",
    )(tile_expert, n_used, xs, wg, wu, wd)


def _combine_kernel(pos_ref, gate_ref, x_ref, gn_ref, ys_hbm, out_ref, buf, sem, *, tc):
    def row_copy(g, u, k, src_row):
        return pltpu.make_async_copy(ys_hbm.at[pl.ds(src_row, 1)], buf.at[k, g, pl.ds(u, 1)], sem)

    def issue(g, c):
        for u in range(SUBLANES):
            for k in range(2):
                row_copy(g, u, k, pos_ref[g, 2 * u + k]).start(priority=k)
        return c

    lax.fori_loop(0, tc // SUBLANES, issue, 0)

    def drain(g, c):
        for u in range(SUBLANES):
            for k in range(2):
                row_copy(0, 0, k, 0).wait()
        return c

    lax.fori_loop(0, tc // SUBLANES, drain, 0)

    g = gate_ref[...]
    D = x_ref.shape[-1]
    x4 = (x_ref[...] + g[:, 0:1] * buf[0].reshape(tc, D) + g[:, 1:2] * buf[1].reshape(tc, D))
    out_ref[...] = _rms(x4, gn_ref[...], NORM_EPS)


def _combine(pos_g, gates_col, x, gn, ys, tc):
    T, D = x.shape
    kern = functools.partial(_combine_kernel, tc=tc)
    gc = tc // SUBLANES
    return pl.pallas_call(
        kern,
        grid=(T // tc,),
        in_specs=[
            pl.BlockSpec((gc, 2 * SUBLANES), lambda i: (i, 0), memory_space=pltpu.SMEM),
            pl.BlockSpec((tc, 2), lambda i: (i, 0)),
            pl.BlockSpec((tc, D), lambda i: (i, 0)),
            _const_spec((1, D)),
            pl.BlockSpec(memory_space=pl.ANY),
        ],
        out_specs=pl.BlockSpec((tc, D), lambda i: (i, 0)),
        out_shape=jax.ShapeDtypeStruct((T, D), F32),
        scratch_shapes=[pltpu.VMEM((2, gc, SUBLANES, D), F32), pltpu.SemaphoreType.DMA],
        compiler_params=pltpu.CompilerParams(
            dimension_semantics=("arbitrary",),
            vmem_limit_bytes=_vmem_limit(40 * 1024 * 1024)),
        name="combine",
    )(pos_g, gates_col, x, gn, ys)


def _tile(n, pref):
    t = min(n, pref)
    assert n % t == 0, (n, t)
    return t


def kernel(x, norm_mix, norm_ffn, norm_final, rel_bias, attn_w_qkv, attn_w_o, attn_lambda_q1, attn_lambda_k1, attn_lambda_q2, attn_lambda_k2, attn_subln, lru_w_in, lru_conv_w, lru_conv_b, lru_gate_a_w, lru_gate_a_b, lru_gate_x_w, lru_gate_x_b, lru_lambda, lru_w_out, ffn_w_gate, ffn_w_up, ffn_w_down, moe_router, moe_w_gate, moe_w_up, moe_w_down):
    B, S, D = x.shape
    T = B * S
    assert D == N_HEADS * V_DIM
    row = lambda v: v.reshape(1, -1).astype(F32)

    ta = _tile(S, ATTN_TILE)
    wqkv = attn_w_qkv[0]
    wqT = (wqkv[:, :D] * (SCALE * LOG2E)).T.astype(BF16)
    wk = wqkv[:, D:2 * D].astype(BF16)
    wvT = wqkv[:, 2 * D:].T.astype(BF16)
    qT, k, vT = _qkv(x, row(norm_mix[0]), wqT, wk, wvT, ta)
    lqk = jnp.stack([attn_lambda_q1[0], attn_lambda_k1[0],
                     attn_lambda_q2[0], attn_lambda_k2[0]]).astype(F32)
    bias = _bias_tiles(rel_bias, ta)
    o = _attention(lqk, qT, k, vT, bias, attn_subln[0].reshape(V_DIM, 1).astype(F32), ta)

    tm = _tile(T, TOKEN_TILE)
    x2, h2 = _ffn(o.reshape(T, D), x.reshape(T, D), attn_w_o[0].astype(BF16), row(norm_ffn[0]),
                  ffn_w_gate[0].astype(BF16), ffn_w_up[0].astype(BF16),
                  ffn_w_down[0].astype(BF16), row(norm_mix[1]), tm)

    ts = _tile(S, LRU_TILE)
    x3, h3, lg = _lru(h2.reshape(B, S, D), x2.reshape(B, S, D), lru_w_in[0].astype(BF16),
                      lru_conv_w[0].astype(F32), row(lru_conv_b[0]),
                      lru_gate_a_w[0].astype(BF16), row(lru_gate_a_b[0]),
                      lru_gate_x_w[0].astype(BF16), row(lru_gate_x_b[0]),
                      row(lru_lambda[0]), lru_w_out[0].astype(BF16), row(norm_ffn[1]),
                      moe_router[0].T.astype(F32), ts)
    x3 = x3.reshape(T, D)
    h3 = h3.reshape(T, D)
    E = moe_router.shape[-1]
    lgT = jnp.transpose(lg, (1, 0, 2)).reshape(E, T)

    idx, gates, counts = _route(lgT, _tile(T, ROUTE_TILE))
    rm = EXPERT_ROW_TILE
    cnt = counts[:, 0]
    padded = ((cnt + rm - 1) // rm) * rm
    ends = jnp.cumsum(padded)
    starts = ends - padded
    pos = jnp.stack([starts[idx[0]] + idx[2], starts[idx[1]] + idx[3]])
    n_tiles = (2 * T) // rm + E
    tile_rows = jnp.arange(n_tiles, dtype=jnp.int32) * rm
    tile_expert = jnp.minimum(
        jnp.sum((tile_rows[:, None] >= ends[None, :]).astype(jnp.int32), axis=1), E - 1)
    n_used = (ends[-1] // rm).astype(jnp.int32).reshape(1)

    last_tile = jnp.maximum(ends // rm - 1, 0)
    tail_tile = jnp.minimum(n_used[0] + jnp.arange(E, dtype=jnp.int32), n_tiles - 1)
    zero_tiles = jnp.concatenate([last_tile, tail_tile]).astype(jnp.int32)

    td = _tile(T, DISPATCH_TILE)
    pos_g = pos.T.reshape(T // SUBLANES, 2 * SUBLANES)
    xs = _dispatch(zero_tiles, pos_g, h3, n_tiles * rm, td, rm)
    ys = _experts(tile_expert, n_used, xs, moe_w_gate[0].astype(BF16),
                  moe_w_up[0].astype(BF16), moe_w_down[0].astype(BF16), rm, EXPERT_FF_CHUNKS)
    out = _combine(pos_g, gates[:2].T, x3, row(norm_final), ys, td)
    return out.reshape(B, S, D)
```

```python
import functools
import math

import numpy as np
import jax
import jax.numpy as jnp
from jax import lax
from jax.experimental import pallas as pl
from jax.experimental.pallas import tpu as pltpu

F32 = jnp.float32
BF16 = jnp.bfloat16

N_HEADS = 8
HEAD_DIM = 64
V_DIM = 2 * HEAD_DIM
SCALE = HEAD_DIM ** -0.5
LOG2E = math.log2(math.e)
NEG_INF = -1e30
NUM_BUCKETS = 32
MAX_EXACT = NUM_BUCKETS // 2
MAX_DISTANCE = 128
N_LRU_BLOCKS = 10
LRU_BLOCK = 128
CONV_WIDTH = 4
LRU_C = 8.0
N_EXPERTS = 8
NORM_EPS = 1e-6
SUBLN_EPS = 1e-5
LAM_INIT_0 = 0.8 - 0.6 * math.exp(-0.3 * 0)

V7X_VMEM_BYTES = 64 * 1024 * 1024
SUBLANES = 8
BF16_ROWS = 16

ATTN_TILE = 512
ATTN_UNROLL = 4
TOKEN_TILE = 512
LRU_TILE = 256
ROUTE_TILE = 1024
EXPERT_ROW_TILE = 1024
EXPERT_FF_CHUNKS = 2
FF_SLAB = 256
PROJ_SLAB = 256
DISPATCH_TILE = 512

NT_DIMS = (((1,), (1,)), ((), ()))


def _vmem_limit(nbytes):
    return int(min(nbytes, V7X_VMEM_BYTES - 6 * 1024 * 1024))


def _rms(x, g, eps):
    ms = jnp.mean(x * x, axis=-1, keepdims=True)
    return x * lax.rsqrt(ms + eps) * g


def _const_spec(shape):
    nd = len(shape)
    return pl.BlockSpec(shape, lambda *_: (0,) * nd, pipeline_mode=pl.Buffered(1))


def _qkv_kernel(x_ref, g_ref, wqT_ref, wk_ref, wvT_ref, qT_ref, k_ref, vT_ref):
    hn = _rms(x_ref[0], g_ref[...], NORM_EPS).astype(BF16)
    qT_ref[0, 0] = lax.dot_general(wqT_ref[...], hn, NT_DIMS,
                                   preferred_element_type=F32).astype(BF16)
    k_ref[0] = jnp.dot(hn, wk_ref[...], preferred_element_type=F32).astype(BF16)
    vT_ref[0, 0] = lax.dot_general(wvT_ref[...], hn, NT_DIMS,
                                   preferred_element_type=F32).astype(BF16)


def _qkv(x, g, wqT, wk, wvT, tm):
    B, S, D = x.shape
    n = S // tm
    return pl.pallas_call(
        _qkv_kernel,
        grid=(B, n),
        in_specs=[
            pl.BlockSpec((1, tm, D), lambda b, i: (b, i, 0)),
            _const_spec((1, D)),
            _const_spec((D, D)),
            _const_spec((D, D)),
            _const_spec((D, D)),
        ],
        out_specs=[
            pl.BlockSpec((1, 1, D, tm), lambda b, i: (b, i, 0, 0)),
            pl.BlockSpec((1, tm, D), lambda b, i: (b, i, 0)),
            pl.BlockSpec((1, 1, D, tm), lambda b, i: (b, i, 0, 0)),
        ],
        out_shape=[
            jax.ShapeDtypeStruct((B, n, D, tm), BF16),
            jax.ShapeDtypeStruct((B, S, D), BF16),
            jax.ShapeDtypeStruct((B, n, D, tm), BF16),
        ],
        compiler_params=pltpu.CompilerParams(
            dimension_semantics=("arbitrary", "arbitrary"),
            vmem_limit_bytes=_vmem_limit(40 * 1024 * 1024)),
        name="qkv",
    )(x, g, wqT, wk, wvT)


def _t5_bucket_np(n):
    nf = np.maximum(n, 1).astype(np.float32)
    large = MAX_EXACT + (np.log(nf / np.float32(MAX_EXACT))
                         / np.float32(math.log(MAX_DISTANCE / MAX_EXACT))
                         * np.float32(NUM_BUCKETS - MAX_EXACT)).astype(np.int32)
    large = np.minimum(large, NUM_BUCKETS - 1)
    return np.where(n < MAX_EXACT, n, large)


def _bias_kernel(w_ref, out_ref, *, t):
    full = jnp.broadcast_to(w_ref[0], (t, 4 * t))
    c = pltpu.roll(full, 0, axis=1, stride=1, stride_axis=0)
    out_ref[0, 0] = c[:, :t]
    out_ref[0, 1] = c[:, t:2 * t]


def _bias_tiles(rel_bias, t):
    assert t >= MAX_DISTANCE and _t5_bucket_np(np.array([t + 1]))[0] == NUM_BUCKETS - 1
    H = rel_bias.shape[1]
    rb = rel_bias.astype(F32) - rel_bias[NUM_BUCKETS - 1].astype(F32)[None, :]
    near = jnp.take(rb, jnp.asarray(_t5_bucket_np(np.arange(2 * t))), axis=0).T
    w = jnp.concatenate([near, jnp.full((H, 2 * t), NEG_INF, F32)], axis=1) * LOG2E
    return pl.pallas_call(
        functools.partial(_bias_kernel, t=t),
        grid=(H,),
        in_specs=[pl.BlockSpec((1, 1, 4 * t), lambda h: (h, 0, 0))],
        out_specs=pl.BlockSpec((1, 2, t, t), lambda h: (h, 0, 0, 0)),
        out_shape=jax.ShapeDtypeStruct((H, 2, t, t), F32),
        compiler_params=pltpu.CompilerParams(dimension_semantics=("arbitrary",)),
        name="bias",
    )(w.reshape(H, 1, 4 * t))


def _attn_kernel(lqk_ref, qT_ref, k_ref, vT_ref, bias_ref, g_ref, o_ref,
                 q_scr, sa_scr, sb_scr, sc_scr, mba_scr, mbb_scr, mbc_scr, m_scr, acc_scr,
                 *, t, n):
    row = lax.broadcasted_iota(jnp.int32, (V_DIM, t), 0)
    for i in range(n):
        qT = qT_ref[0, i]
        zero = jnp.zeros_like(qT)
        q_scr[i, 0] = jnp.where(row < HEAD_DIM, qT, zero)
        q_scr[i, 1] = jnp.where(row >= HEAD_DIM, qT, zero)

    lqk = lqk_ref[...]
    e1 = jnp.exp(jnp.sum(lqk[0:1] * lqk[1:2], axis=-1, keepdims=True))
    e2 = jnp.exp(jnp.sum(lqk[2:3] * lqk[3:4], axis=-1, keepdims=True))
    lam = e1 - e2 + LAM_INIT_0

    def reset():
        m_scr[...] = jnp.full(m_scr.shape, NEG_INF, F32)
        acc_scr[...] = jnp.zeros(acc_scr.shape, F32)

    def scores(qi, j, bias_idx, s_scr, mb_scr):
        kc = k_ref[0, pl.ds(pl.multiple_of(j * t, t), t), :]
        for c in range(2):
            s = jnp.dot(kc, q_scr[qi, c], preferred_element_type=F32)
            if bias_idx is not None:
                s = s + bias_ref[0, bias_idx]
            s_scr[c] = s
            mb_scr[c] = jnp.max(s, axis=0, keepdims=True)

    ones_rows = jnp.where(lax.broadcasted_iota(jnp.int32, (BF16_ROWS, t), 0) == 0, 1.0, 0.0).astype(BF16)

    def consume(j, s_scr, mb_scr):
        vt = jnp.concatenate([vT_ref[0, j], ones_rows], axis=0)
        for c in range(2):
            m_old = m_scr[c]
            m_new = jnp.maximum(m_old, mb_scr[c])
            alpha = jnp.exp2(m_old - m_new)
            m_scr[c] = m_new
            p = jnp.exp2(s_scr[c] - m_new).astype(BF16)
            acc_scr[c] = alpha * acc_scr[c] + jnp.dot(vt, p, preferred_element_type=F32)

    def finalize(qi):
        l0 = acc_scr[0, V_DIM:V_DIM + 1, :]
        l1 = acc_scr[1, V_DIM:V_DIM + 1, :]
        o = acc_scr[0, :V_DIM, :] / l0 - lam * (acc_scr[1, :V_DIM, :] / l1)
        ms = jnp.mean(o * o, axis=0, keepdims=True)
        on = o * lax.rsqrt(ms + SUBLN_EPS) * g_ref[...] * (1.0 - LAM_INIT_0)
        o_ref[0, pl.ds(pl.multiple_of(qi * t, t), t), :] = on.T.astype(BF16)

    a_bufs = (sa_scr, mba_scr)
    b_bufs = (sb_scr, mbb_scr)
    c_bufs = (sc_scr, mbc_scr)

    reset()
    scores(0, 0, 0, *c_bufs)
    consume(0, *c_bufs)
    finalize(0)
    if n == 1:
        return
    scores(1, 1, 0, *c_bufs)

    def tile(qi, carry):
        reset()
        scores(qi, qi - 1, 1, *a_bufs)
        consume(qi, *c_bufs)
        n_far = qi - 1

        def far_steps(first, count):
            for s in range(count):
                into, held = (b_bufs, a_bufs) if s % 2 == 0 else (a_bufs, b_bufs)
                scores(qi, first - 1 - s, None, *into)
                consume(first - s, *held)

        def trip(pp, c2):
            far_steps(n_far - ATTN_UNROLL * pp, ATTN_UNROLL)
            return c2

        lax.fori_loop(0, n_far // ATTN_UNROLL, trip, 0)
        nxt = jnp.minimum(qi + 1, n - 1)
        rem = n_far % ATTN_UNROLL

        for r in range(ATTN_UNROLL):
            @pl.when(rem == r)
            def _(r=r):
                far_steps(r, r)
                scores(nxt, nxt, 0, *c_bufs)
                consume(0, *(a_bufs if r % 2 == 0 else b_bufs))

        finalize(qi)
        return carry

    lax.fori_loop(1, n, tile, 0)


def _attention(lqk, qT, k, vT, bias, g_col, t):
    B, n, D, _ = qT.shape
    S = n * t
    H = D // V_DIM
    kern = functools.partial(_attn_kernel, t=t, n=n)
    return pl.pallas_call(
        kern,
        grid=(B, H),
        in_specs=[
            _const_spec(lqk.shape),
            pl.BlockSpec((1, n, V_DIM, t), lambda b, h: (b, 0, h, 0)),
            pl.BlockSpec((1, S, V_DIM), lambda b, h: (b, 0, h)),
            pl.BlockSpec((1, n, V_DIM, t), lambda b, h: (b, 0, h, 0)),
            pl.BlockSpec((1, 2, t, t), lambda b, h: (h, 0, 0, 0)),
            _const_spec((V_DIM, 1)),
        ],
        out_specs=pl.BlockSpec((1, S, V_DIM), lambda b, h: (b, 0, h)),
        out_shape=jax.ShapeDtypeStruct((B, S, D), BF16),
        scratch_shapes=[
            pltpu.VMEM((n, 2, V_DIM, t), BF16),
            pltpu.VMEM((2, t, t), F32),
            pltpu.VMEM((2, t, t), F32),
            pltpu.VMEM((2, t, t), F32),
            pltpu.VMEM((2, 1, t), F32),
            pltpu.VMEM((2, 1, t), F32),
            pltpu.VMEM((2, 1, t), F32),
            pltpu.VMEM((2, 1, t), F32),
            pltpu.VMEM((2, V_DIM + BF16_ROWS, t), F32),
        ],
        compiler_params=pltpu.CompilerParams(
            dimension_semantics=("arbitrary", "arbitrary"),
            vmem_limit_bytes=_vmem_limit(52 * 1024 * 1024)),
        name="attn",
    )(lqk, qT, k, vT, bias, g_col)


def _swiglu_acc(h, wg, wu, wd, acc):
    width = wg.shape[-1]
    step = FF_SLAB if width % FF_SLAB == 0 else width
    for c in range(width // step):
        sl = slice(c * step, (c + 1) * step)
        gt = jnp.dot(h, wg[:, sl], preferred_element_type=F32)
        up = jnp.dot(h, wu[:, sl], preferred_element_type=F32)
        a = (gt * jax.nn.sigmoid(gt) * up).astype(BF16)
        acc = acc + jnp.dot(a, wd[sl, :], preferred_element_type=F32)
    return acc


def _ffn_kernel(o_ref, x_ref, wo_ref, g1_ref, wg_ref, wu_ref, wd_ref, g2_ref, x2_ref, h2_ref):
    x1 = x_ref[...] + jnp.dot(o_ref[...], wo_ref[...], preferred_element_type=F32)
    h1 = _rms(x1, g1_ref[...], NORM_EPS).astype(BF16)
    x2 = _swiglu_acc(h1, wg_ref, wu_ref, wd_ref, x1)
    x2_ref[...] = x2
    h2_ref[...] = _rms(x2, g2_ref[...], NORM_EPS).astype(BF16)


def _ffn(o, x, wo, g1, wg, wu, wd, g2, tm):
    T, D = x.shape
    Fd = wg.shape[1]
    kern = _ffn_kernel
    tile = lambda i: (i, 0)
    return pl.pallas_call(
        kern,
        grid=(T // tm,),
        in_specs=[
            pl.BlockSpec((tm, D), tile),
            pl.BlockSpec((tm, D), tile),
            _const_spec((D, D)),
            _const_spec((1, D)),
            _const_spec((D, Fd)),
            _const_spec((D, Fd)),
            _const_spec((Fd, D)),
            _const_spec((1, D)),
        ],
        out_specs=[pl.BlockSpec((tm, D), tile), pl.BlockSpec((tm, D), tile)],
        out_shape=[jax.ShapeDtypeStruct((T, D), F32), jax.ShapeDtypeStruct((T, D), BF16)],
        compiler_params=pltpu.CompilerParams(
            dimension_semantics=("arbitrary",),
            vmem_limit_bytes=_vmem_limit(56 * 1024 * 1024)),
        name="ffn",
    )(o, x, wo, g1, wg, wu, wd, g2)


def _gelu_tanh(x):
    return 0.5 * x * (1.0 + jnp.tanh(math.sqrt(2.0 / math.pi) * (x + 0.044715 * (x * x * x))))


def _lru_kernel(h2_ref, h2n_ref, x2_ref, win_ref, cw_ref, cb_ref, gaw_ref, gab_ref, gxw_ref,
                gxb_ref, lam_ref, wout_ref, gn_ref, rT_ref, x3_ref, h3_ref, lg_ref,
                gx0_scr, gx1_scr, xbuf, a_scr, b_scr, h_scr, hcar, *, ts, dr):
    si = pl.program_id(1)
    halo = SUBLANES

    @pl.when(si == 0)
    def _():
        xbuf[0:halo, :] = jnp.zeros((halo, dr), F32)
        hcar[...] = jnp.zeros(hcar.shape, F32)

    @pl.when((pl.program_id(0) == 0) & (si == 0))
    def _():
        gx0_scr[...] = jnp.dot(h2_ref[0, 0:ts, :], win_ref[...], preferred_element_type=F32)

    z = -lam_ref[...]
    ez = jnp.exp(-jnp.abs(z))
    u = 1.0 + ez
    du = u - 1.0
    log1p_ez = jnp.where(du == 0.0, ez, jnp.log(u) * (ez / du))
    softplus = jnp.maximum(z, 0.0) + log1p_ez

    def in_proj_slabs(load_rows, gx_scr):
        def slab(c0):
            def emit():
                gx_scr[:, c0:c0 + PROJ_SLAB] = jnp.dot(
                    load_rows(), win_ref[:, c0:c0 + PROJ_SLAB], preferred_element_type=F32)
            return emit
        return [slab(c0) for c0 in range(0, 2 * dr, PROJ_SLAB)]

    _lru_tile(0, gx0_scr, in_proj_slabs(lambda: h2_ref[0, ts:2 * ts, :], gx1_scr), softplus,
              x2_ref, cw_ref, cb_ref, gaw_ref, gab_ref, gxw_ref, gxb_ref,
              wout_ref, gn_ref, rT_ref, x3_ref, h3_ref, lg_ref, xbuf, a_scr, b_scr, h_scr, hcar,
              ts=ts, dr=dr)
    _lru_tile(ts, gx1_scr, in_proj_slabs(lambda: h2n_ref[0], gx0_scr), softplus,
              x2_ref, cw_ref, cb_ref, gaw_ref, gab_ref, gxw_ref, gxb_ref,
              wout_ref, gn_ref, rT_ref, x3_ref, h3_ref, lg_ref, xbuf, a_scr, b_scr, h_scr, hcar,
              ts=ts, dr=dr)


def _lru_tile(r0, gx_scr, background, softplus, x2_ref, cw_ref, cb_ref, gaw_ref, gab_ref, gxw_ref,
              gxb_ref, wout_ref, gn_ref, rT_ref, x3_ref, h3_ref, lg_ref, xbuf, a_scr, b_scr, h_scr,
              hcar, *, ts, dr):
    halo = SUBLANES
    pending = list(background)
    per_phase = -(-len(pending) // 10)

    def emit_background():
        for _ in range(min(per_phase, len(pending))):
            pending.pop(0)()

    emit_background()
    gate = _gelu_tanh(gx_scr[:, :dr])
    xb = gx_scr[:, dr:]
    xbuf[halo:halo + ts, :] = xb

    emit_background()
    cw = cw_ref[...]
    xc = cb_ref[...] + cw[CONV_WIDTH - 1:CONV_WIDTH] * xb
    for j in range(CONV_WIDTH - 1):
        back = CONV_WIDTH - 1 - j
        xc = xc + cw[j:j + 1] * xbuf[halo - back:halo - back + ts, :]
    xbuf[0:halo, :] = xb[ts - halo:ts, :]

    emit_background()
    xcb = xc.astype(BF16)
    r_parts, i_parts = [], []
    for n in range(dr // LRU_BLOCK):
        blk = xcb[:, n * LRU_BLOCK:(n + 1) * LRU_BLOCK]
        r_parts.append(jnp.dot(blk, gaw_ref[n], preferred_element_type=F32))
        i_parts.append(jnp.dot(blk, gxw_ref[n], preferred_element_type=F32))
    emit_background()
    r = jax.nn.sigmoid(jnp.concatenate(r_parts, axis=1) + gab_ref[...])
    emit_background()
    ig = jax.nn.sigmoid(jnp.concatenate(i_parts, axis=1) + gxb_ref[...])
    emit_background()
    log_a = (-LRU_C) * r * softplus
    a = jnp.exp(log_a)
    b = jnp.sqrt(1.0 - a * a) * (ig * xc)

    emit_background()
    ng = ts // SUBLANES
    a3 = a.reshape(ng, SUBLANES, dr)
    b3 = b.reshape(ng, SUBLANES, dr)
    sub = lax.broadcasted_iota(jnp.int32, a3.shape, 1)
    d = 1
    while d < SUBLANES:
        keep = sub >= d
        a_s = pltpu.roll(a3, d, axis=1)
        b_s = pltpu.roll(b3, d, axis=1)
        b3 = jnp.where(keep, a3 * b_s + b3, b3)
        a3 = jnp.where(keep, a3 * a_s, a3)
        d *= 2
        emit_background()
    a_scr[...] = a3.reshape(ts, dr)
    b_scr[...] = b3.reshape(ts, dr)

    emit_background()
    assert not pending
    hprev = hcar[...]
    for g in range(ng):
        rows = slice(g * SUBLANES, (g + 1) * SUBLANES)
        hg = a_scr[rows, :] * hprev + b_scr[rows, :]
        h_scr[rows, :] = hg
        hprev = jnp.broadcast_to(hg[SUBLANES - 1:SUBLANES, :], (SUBLANES, dr))
    hcar[...] = hprev

    y = (h_scr[...] * gate).astype(BF16)
    x3 = x2_ref[0, r0:r0 + ts, :] + jnp.dot(y, wout_ref[...], preferred_element_type=F32)
    x3_ref[0, r0:r0 + ts, :] = x3

    h3 = _rms(x3, gn_ref[...], NORM_EPS)
    h3_ref[0, r0:r0 + ts, :] = h3
    h_hi = h3.astype(BF16)
    h_lo = (h3 - h_hi.astype(F32)).astype(BF16)
    rT = rT_ref[...]
    r_hi = rT.astype(BF16)
    r_lo = (rT - r_hi.astype(F32)).astype(BF16)
    lg = lax.dot_general(r_hi, h_hi, NT_DIMS, preferred_element_type=F32)
    lg = lg + lax.dot_general(r_lo, h_hi, NT_DIMS, preferred_element_type=F32)
    lg = lg + lax.dot_general(r_hi, h_lo, NT_DIMS, preferred_element_type=F32)
    lg_ref[0, :, r0:r0 + ts] = lg


def _lru(h2, x2, win, cw, cb, gaw, gab, gxw, gxb, lam, wout, gn, rT, ts):
    B, S, D = x2.shape
    dr = wout.shape[0]
    E = rT.shape[0]
    n = S // (2 * ts)
    kern = functools.partial(_lru_kernel, ts=ts, dr=dr)
    tile = lambda b, i: (b, i, 0)

    def next_first(b, i):
        last = i + 1 >= n
        return (jnp.where(last, jnp.minimum(b + 1, B - 1), b), jnp.where(last, 0, 2 * (i + 1)), 0)

    return pl.pallas_call(
        kern,
        grid=(B, n),
        in_specs=[
            pl.BlockSpec((1, 2 * ts, D), tile),
            pl.BlockSpec((1, ts, D), next_first),
            pl.BlockSpec((1, 2 * ts, D), tile),
            _const_spec(win.shape),
            _const_spec(cw.shape),
            _const_spec(cb.shape),
            _const_spec(gaw.shape),
            _const_spec(gab.shape),
            _const_spec(gxw.shape),
            _const_spec(gxb.shape),
            _const_spec(lam.shape),
            _const_spec(wout.shape),
            _const_spec(gn.shape),
            _const_spec(rT.shape),
        ],
        out_specs=[pl.BlockSpec((1, 2 * ts, D), tile),
                   pl.BlockSpec((1, 2 * ts, D), tile),
                   pl.BlockSpec((1, E, 2 * ts), lambda b, i: (b, 0, i))],
        out_shape=[jax.ShapeDtypeStruct((B, S, D), F32),
                   jax.ShapeDtypeStruct((B, S, D), F32),
                   jax.ShapeDtypeStruct((B, E, S), F32)],
        scratch_shapes=[
            pltpu.VMEM((ts, 2 * dr), F32),
            pltpu.VMEM((ts, 2 * dr), F32),
            pltpu.VMEM((SUBLANES + ts, dr), F32),
            pltpu.VMEM((ts, dr), F32),
            pltpu.VMEM((ts, dr), F32),
            pltpu.VMEM((ts, dr), F32),
            pltpu.VMEM((SUBLANES, dr), F32),
        ],
        compiler_params=pltpu.CompilerParams(
            dimension_semantics=("arbitrary", "arbitrary"),
            vmem_limit_bytes=_vmem_limit(48 * 1024 * 1024)),
        name="lru",
    )(h2, h2, x2, win, cw, cb, gaw, gab, gxw, gxb, lam, wout, gn, rT)


def _route_kernel(lg_ref, idx_ref, gate_ref, cnt_ref, carry, *, tr):
    @pl.when(pl.program_id(0) == 0)
    def _():
        carry[...] = jnp.zeros(carry.shape, F32)

    lg = lg_ref[...]
    E = lg.shape[0]
    eidx = lax.broadcasted_iota(jnp.int32, lg.shape, 0).astype(F32)
    m1 = jnp.max(lg, axis=0, keepdims=True)
    i1 = jnp.min(jnp.where(lg == m1, eidx, float(E)), axis=0, keepdims=True)
    oh1 = eidx == i1
    lg2 = jnp.where(oh1, -jnp.inf, lg)
    m2 = jnp.max(lg2, axis=0, keepdims=True)
    i2 = jnp.min(jnp.where(lg2 == m2, eidx, float(E)), axis=0, keepdims=True)
    oh2 = eidx == i2
    ex = jnp.exp(m2 - m1)
    g1 = 1.0 / (1.0 + ex)
    g2 = ex / (1.0 + ex)

    both = jnp.where(oh1 | oh2, 1.0, 0.0).astype(BF16)
    ti = lax.broadcasted_iota(jnp.int32, (tr, tr), 0)
    tj = lax.broadcasted_iota(jnp.int32, (tr, tr), 1)
    before = jnp.where(ti < tj, 1.0, 0.0).astype(BF16)
    rank = jnp.dot(both, before, preferred_element_type=F32) + carry[...][:, 0:1]
    r1 = jnp.sum(jnp.where(oh1, rank, 0.0), axis=0, keepdims=True)
    r2 = jnp.sum(jnp.where(oh2, rank, 0.0), axis=0, keepdims=True)
    tot = carry[...] + jnp.sum(both.astype(F32), axis=1, keepdims=True)
    carry[...] = tot
    cnt_ref[...] = tot.astype(jnp.int32)

    orow = lax.broadcasted_iota(jnp.int32, (8, tr), 0)
    packed = jnp.where(orow == 0, i1, jnp.where(orow == 1, i2, jnp.where(orow == 2, r1, r2)))
    idx_ref[...] = packed.astype(jnp.int32)
    gate_ref[...] = jnp.where(orow == 0, g1, g2)


def _route(lgT, tr):
    E, T = lgT.shape
    kern = functools.partial(_route_kernel, tr=tr)
    return pl.pallas_call(
        kern,
        grid=(T // tr,),
        in_specs=[pl.BlockSpec((E, tr), lambda i: (0, i))],
        out_specs=[pl.BlockSpec((8, tr), lambda i: (0, i)),
                   pl.BlockSpec((8, tr), lambda i: (0, i)),
                   pl.BlockSpec((E, 128), lambda i: (0, 0))],
        out_shape=[jax.ShapeDtypeStruct((8, T), jnp.int32),
                   jax.ShapeDtypeStruct((8, T), F32),
                   jax.ShapeDtypeStruct((E, 128), jnp.int32)],
        scratch_shapes=[pltpu.VMEM((E, 128), F32)],
        compiler_params=pltpu.CompilerParams(dimension_semantics=("arbitrary",)),
        name="route",
    )(lgT)


def _dispatch_kernel(zt_ref, pos_ref, x_ref, xs_out, zero_scr, sem, zsem, *, td, rm):

    @pl.when(pl.program_id(0) == 0)
    def _():
        zero_scr[...] = jnp.zeros(zero_scr.shape, F32)
        for i in range(zt_ref.shape[0]):
            dst = xs_out.at[pl.ds(pl.multiple_of(zt_ref[i] * rm, rm), rm)]
            cp = pltpu.make_async_copy(zero_scr, dst, zsem)
            cp.start()
            cp.wait()

    def row_copy(g, u, k, dst_row):
        return pltpu.make_async_copy(x_ref.at[g, pl.ds(u, 1)], xs_out.at[pl.ds(dst_row, 1)], sem)

    def issue(g, c):
        for u in range(SUBLANES):
            for k in range(2):
                row_copy(g, u, k, pos_ref[g, 2 * u + k]).start(priority=k)
        return c

    lax.fori_loop(0, td // SUBLANES, issue, 0)

    def drain(g, c):
        for u in range(SUBLANES):
            for k in range(2):
                row_copy(0, 0, k, 0).wait()
        return c

    lax.fori_loop(0, td // SUBLANES, drain, 0)


def _dispatch(zero_tiles, pos_g, x, n_rows, td, rm):
    T, D = x.shape
    kern = functools.partial(_dispatch_kernel, td=td, rm=rm)
    gd = td // SUBLANES
    return pl.pallas_call(
        kern,
        grid_spec=pltpu.PrefetchScalarGridSpec(
            num_scalar_prefetch=1,
            grid=(T // td,),
            in_specs=[
                pl.BlockSpec((gd, 2 * SUBLANES), lambda i, zt: (i, 0), memory_space=pltpu.SMEM),
                pl.BlockSpec((gd, SUBLANES, D), lambda i, zt: (i, 0, 0)),
            ],
            out_specs=pl.BlockSpec(memory_space=pl.ANY),
            scratch_shapes=[pltpu.VMEM((rm, D), F32), pltpu.SemaphoreType.DMA,
                            pltpu.SemaphoreType.DMA],
        ),
        out_shape=jax.ShapeDtypeStruct((n_rows, D), F32),
        compiler_params=pltpu.CompilerParams(
            dimension_semantics=("arbitrary",), has_side_effects=True),
        name="dispatch",
    )(zero_tiles, pos_g, x.reshape(T // SUBLANES, SUBLANES, D))


def _expert_kernel(te_ref, nu_ref, xs_ref, wg_ref, wu_ref, wd_ref, ys_ref, *, nf):
    i = pl.program_id(0)
    j = pl.program_id(1)

    @pl.when(j == 0)
    def _():
        ys_ref[...] = jnp.zeros(ys_ref.shape, F32)

    @pl.when(i < nu_ref[0])
    def _():
        ys_ref[...] = _swiglu_acc(xs_ref[...].astype(BF16), wg_ref.at[0], wu_ref.at[0],
                                  wd_ref.at[0], ys_ref[...])


def _experts(tile_expert, n_used, xs, wg, wu, wd, tm, nf):
    NP, D = xs.shape
    E, _, Fd = wg.shape
    fc = Fd // nf
    nt = NP // tm

    def eff(i, nu):
        return jnp.minimum(i, nu[0] - 1)

    def chunk(i, j, nu):
        ie = eff(i, nu)
        je = jnp.where(i < nu[0], j, nf - 1)
        return jnp.where(ie % 2 == 0, je, nf - 1 - je)

    kern = functools.partial(_expert_kernel, nf=nf)
    return pl.pallas_call(
        kern,
        grid_spec=pltpu.PrefetchScalarGridSpec(
            num_scalar_prefetch=2,
            grid=(nt, nf),
            in_specs=[
                pl.BlockSpec((tm, D), lambda i, j, te, nu: (eff(i, nu), 0)),
                pl.BlockSpec((1, D, fc), lambda i, j, te, nu: (te[eff(i, nu)], 0, chunk(i, j, nu))),
                pl.BlockSpec((1, D, fc), lambda i, j, te, nu: (te[eff(i, nu)], 0, chunk(i, j, nu))),
                pl.BlockSpec((1, fc, D), lambda i, j, te, nu: (te[eff(i, nu)], chunk(i, j, nu), 0)),
            ],
            out_specs=pl.BlockSpec((tm, D), lambda i, j, te, nu: (i, 0)),
        ),
        out_shape=jax.ShapeDtypeStruct((NP, D), F32),
        compiler_params=pltpu.CompilerParams(
            dimension_semantics=("arbitrary", "arbitrary"),
            vmem_limit_bytes=_vmem_limit(56 * 1024 * 1024)),
        name="experts",
    )(tile_expert, n_used, xs, wg, wu, wd)


def _combine_kernel(pos_ref, gate_ref, x_ref, gn_ref, ys_hbm, out_ref, buf, sem, *, tc):
    def row_copy(g, u, k, src_row):
        return pltpu.make_async_copy(ys_hbm.at[pl.ds(src_row, 1)], buf.at[k, g, pl.ds(u, 1)], sem)

    def issue(g, c):
        for u in range(SUBLANES):
            for k in range(2):
                row_copy(g, u, k, pos_ref[g, 2 * u + k]).start(priority=k)
        return c

    lax.fori_loop(0, tc // SUBLANES, issue, 0)

    def drain(g, c):
        for u in range(SUBLANES):
            for k in range(2):
                row_copy(0, 0, k, 0).wait()
        return c

    lax.fori_loop(0, tc // SUBLANES, drain, 0)

    g = gate_ref[...]
    D = x_ref.shape[-1]
    x4 = (x_ref[...] + g[:, 0:1] * buf[0].reshape(tc, D) + g[:, 1:2] * buf[1].reshape(tc, D))
    out_ref[...] = _rms(x4, gn_ref[...], NORM_EPS)


def _combine(pos_g, gates_col, x, gn, ys, tc):
    T, D = x.shape
    kern = functools.partial(_combine_kernel, tc=tc)
    gc = tc // SUBLANES
    return pl.pallas_call(
        kern,
        grid=(T // tc,),
        in_specs=[
            pl.BlockSpec((gc, 2 * SUBLANES), lambda i: (i, 0), memory_space=pltpu.SMEM),
            pl.BlockSpec((tc, 2), lambda i: (i, 0)),
            pl.BlockSpec((tc, D), lambda i: (i, 0)),
            _const_spec((1, D)),
            pl.BlockSpec(memory_space=pl.ANY),
        ],
        out_specs=pl.BlockSpec((tc, D), lambda i: (i, 0)),
        out_shape=jax.ShapeDtypeStruct((T, D), F32),
        scratch_shapes=[pltpu.VMEM((2, gc, SUBLANES, D), F32), pltpu.SemaphoreType.DMA],
        compiler_params=pltpu.CompilerParams(
            dimension_semantics=("arbitrary",),
            vmem_limit_bytes=_vmem_limit(40 * 1024 * 1024)),
        name="combine",
    )(pos_g, gates_col, x, gn, ys)


def _tile(n, pref):
    t = min(n, pref)
    assert n % t == 0, (n, t)
    return t


def kernel(x, norm_mix, norm_ffn, norm_final, rel_bias, attn_w_qkv, attn_w_o, attn_lambda_q1, attn_lambda_k1, attn_lambda_q2, attn_lambda_k2, attn_subln, lru_w_in, lru_conv_w, lru_conv_b, lru_gate_a_w, lru_gate_a_b, lru_gate_x_w, lru_gate_x_b, lru_lambda, lru_w_out, ffn_w_gate, ffn_w_up, ffn_w_down, moe_router, moe_w_gate, moe_w_up, moe_w_down):
    B, S, D = x.shape
    T = B * S
    assert D == N_HEADS * V_DIM
    row = lambda v: v.reshape(1, -1).astype(F32)

    ta = _tile(S, ATTN_TILE)
    wqkv = attn_w_qkv[0]
    wqT = (wqkv[:, :D] * (SCALE * LOG2E)).T.astype(BF16)
    wk = wqkv[:, D:2 * D].astype(BF16)
    wvT = wqkv[:, 2 * D:].T.astype(BF16)
    qT, k, vT = _qkv(x, row(norm_mix[0]), wqT, wk, wvT, ta)
    lqk = jnp.stack([attn_lambda_q1[0], attn_lambda_k1[0],
                     attn_lambda_q2[0], attn_lambda_k2[0]]).astype(F32)
    bias = _bias_tiles(rel_bias, ta)
    o = _attention(lqk, qT, k, vT, bias, attn_subln[0].reshape(V_DIM, 1).astype(F32), ta)

    tm = _tile(T, TOKEN_TILE)
    x2, h2 = _ffn(o.reshape(T, D), x.reshape(T, D), attn_w_o[0].astype(BF16), row(norm_ffn[0]),
                  ffn_w_gate[0].astype(BF16), ffn_w_up[0].astype(BF16),
                  ffn_w_down[0].astype(BF16), row(norm_mix[1]), tm)

    ts = _tile(S, LRU_TILE)
    x3, h3, lg = _lru(h2.reshape(B, S, D), x2.reshape(B, S, D), lru_w_in[0].astype(BF16),
                      lru_conv_w[0].astype(F32), row(lru_conv_b[0]),
                      lru_gate_a_w[0].astype(BF16), row(lru_gate_a_b[0]),
                      lru_gate_x_w[0].astype(BF16), row(lru_gate_x_b[0]),
                      row(lru_lambda[0]), lru_w_out[0].astype(BF16), row(norm_ffn[1]),
                      moe_router[0].T.astype(F32), ts)
    x3 = x3.reshape(T, D)
    h3 = h3.reshape(T, D)
    E = moe_router.shape[-1]
    lgT = jnp.transpose(lg, (1, 0, 2)).reshape(E, T)

    idx, gates, counts = _route(lgT, _tile(T, ROUTE_TILE))
    rm = EXPERT_ROW_TILE
    cnt = counts[:, 0]
    padded = ((cnt + rm - 1) // rm) * rm
    ends = jnp.cumsum(padded)
    starts = ends - padded
    pos = jnp.stack([starts[idx[0]] + idx[2], starts[idx[1]] + idx[3]])
    n_tiles = (2 * T) // rm + E
    tile_rows = jnp.arange(n_tiles, dtype=jnp.int32) * rm
    tile_expert = jnp.minimum(
        jnp.sum((tile_rows[:, None] >= ends[None, :]).astype(jnp.int32), axis=1), E - 1)
    n_used = (ends[-1] // rm).astype(jnp.int32).reshape(1)

    last_tile = jnp.maximum(ends // rm - 1, 0)
    tail_tile = jnp.minimum(n_used[0] + jnp.arange(E, dtype=jnp.int32), n_tiles - 1)
    zero_tiles = jnp.concatenate([last_tile, tail_tile]).astype(jnp.int32)

    td = _tile(T, DISPATCH_TILE)
    pos_g = pos.T.reshape(T // SUBLANES, 2 * SUBLANES)
    xs = _dispatch(zero_tiles, pos_g, h3, n_tiles * rm, td, rm)
    ys = _experts(tile_expert, n_used, xs, moe_w_gate[0].astype(BF16),
                  moe_w_up[0].astype(BF16), moe_w_down[0].astype(BF16), rm, EXPERT_FF_CHUNKS)
    out = _combine(pos_g, gates[:2].T, x3, row(norm_final), ys, td)
    return out.reshape(B, S, D)
```

```python
import functools
import math

import numpy as np
import jax
import jax.numpy as jnp
from jax import lax
from jax.experimental import pallas as pl
from jax.experimental.pallas import tpu as pltpu

F32 = jnp.float32
BF16 = jnp.bfloat16

N_HEADS = 8
HEAD_DIM = 64
V_DIM = 2 * HEAD_DIM
SCALE = HEAD_DIM ** -0.5
LOG2E = math.log2(math.e)
NEG_INF = -1e30
NUM_BUCKETS = 32
MAX_EXACT = NUM_BUCKETS // 2
MAX_DISTANCE = 128
N_LRU_BLOCKS = 10
LRU_BLOCK = 128
CONV_WIDTH = 4
LRU_C = 8.0
N_EXPERTS = 8
NORM_EPS = 1e-6
SUBLN_EPS = 1e-5
LAM_INIT_0 = 0.8 - 0.6 * math.exp(-0.3 * 0)

V7X_VMEM_BYTES = 64 * 1024 * 1024
SUBLANES = 8
BF16_ROWS = 16

ATTN_TILE = 512
ATTN_UNROLL = 6
TOKEN_TILE = 512
LRU_TILE = 256
ROUTE_TILE = 1024
EXPERT_ROW_TILE = 512
EXPERT_FF_CHUNKS = 2
FF_SLAB = 256
PROJ_SLAB = 256
DISPATCH_TILE = 1024

NT_DIMS = (((1,), (1,)), ((), ()))


def _vmem_limit(nbytes):
    return int(min(nbytes, V7X_VMEM_BYTES - 6 * 1024 * 1024))


def _rms(x, g, eps):
    ms = jnp.mean(x * x, axis=-1, keepdims=True)
    return x * lax.rsqrt(ms + eps) * g


def _const_spec(shape):
    nd = len(shape)
    return pl.BlockSpec(shape, lambda *_: (0,) * nd, pipeline_mode=pl.Buffered(1))


def _qkv_kernel(x_ref, g_ref, wqT_ref, wk_ref, wvT_ref, qT_ref, k_ref, vT_ref):
    hn = _rms(x_ref[0], g_ref[...], NORM_EPS).astype(BF16)
    qT_ref[0, 0] = lax.dot_general(wqT_ref[...], hn, NT_DIMS,
                                   preferred_element_type=F32).astype(BF16)
    k_ref[0] = jnp.dot(hn, wk_ref[...], preferred_element_type=F32).astype(BF16)
    vT_ref[0, 0] = lax.dot_general(wvT_ref[...], hn, NT_DIMS,
                                   preferred_element_type=F32).astype(BF16)


def _qkv(x, g, wqT, wk, wvT, tm):
    B, S, D = x.shape
    n = S // tm
    return pl.pallas_call(
        _qkv_kernel,
        grid=(B, n),
        in_specs=[
            pl.BlockSpec((1, tm, D), lambda b, i: (b, i, 0)),
            _const_spec((1, D)),
            _const_spec((D, D)),
            _const_spec((D, D)),
            _const_spec((D, D)),
        ],
        out_specs=[
            pl.BlockSpec((1, 1, D, tm), lambda b, i: (b, i, 0, 0)),
            pl.BlockSpec((1, tm, D), lambda b, i: (b, i, 0)),
            pl.BlockSpec((1, 1, D, tm), lambda b, i: (b, i, 0, 0)),
        ],
        out_shape=[
            jax.ShapeDtypeStruct((B, n, D, tm), BF16),
            jax.ShapeDtypeStruct((B, S, D), BF16),
            jax.ShapeDtypeStruct((B, n, D, tm), BF16),
        ],
        compiler_params=pltpu.CompilerParams(
            dimension_semantics=("arbitrary", "arbitrary"),
            vmem_limit_bytes=_vmem_limit(40 * 1024 * 1024)),
        name="qkv",
    )(x, g, wqT, wk, wvT)


def _t5_bucket_np(n):
    nf = np.maximum(n, 1).astype(np.float32)
    large = MAX_EXACT + (np.log(nf / np.float32(MAX_EXACT))
                         / np.float32(math.log(MAX_DISTANCE / MAX_EXACT))
                         * np.float32(NUM_BUCKETS - MAX_EXACT)).astype(np.int32)
    large = np.minimum(large, NUM_BUCKETS - 1)
    return np.where(n < MAX_EXACT, n, large)


def _bias_kernel(w_ref, out_ref, *, t):
    full = jnp.broadcast_to(w_ref[0], (t, 4 * t))
    c = pltpu.roll(full, 0, axis=1, stride=1, stride_axis=0)
    out_ref[0, 0] = c[:, :t]
    out_ref[0, 1] = c[:, t:2 * t]


def _bias_tiles(rel_bias, t):
    assert t >= MAX_DISTANCE and _t5_bucket_np(np.array([t + 1]))[0] == NUM_BUCKETS - 1
    H = rel_bias.shape[1]
    rb = rel_bias.astype(F32) - rel_bias[NUM_BUCKETS - 1].astype(F32)[None, :]
    near = jnp.take(rb, jnp.asarray(_t5_bucket_np(np.arange(2 * t))), axis=0).T
    w = jnp.concatenate([near, jnp.full((H, 2 * t), NEG_INF, F32)], axis=1) * LOG2E
    return pl.pallas_call(
        functools.partial(_bias_kernel, t=t),
        grid=(H,),
        in_specs=[pl.BlockSpec((1, 1, 4 * t), lambda h: (h, 0, 0))],
        out_specs=pl.BlockSpec((1, 2, t, t), lambda h: (h, 0, 0, 0)),
        out_shape=jax.ShapeDtypeStruct((H, 2, t, t), F32),
        compiler_params=pltpu.CompilerParams(dimension_semantics=("arbitrary",)),
        name="bias",
    )(w.reshape(H, 1, 4 * t))


def _attn_kernel(lqk_ref, qT_ref, k_ref, vT_ref, bias_ref, g_ref, o_ref,
                 q_scr, sa_scr, sb_scr, sc_scr, mba_scr, mbb_scr, mbc_scr, m_scr, acc_scr,
                 *, t, n):
    row = lax.broadcasted_iota(jnp.int32, (V_DIM, t), 0)
    for i in range(n):
        qT = qT_ref[0, i]
        zero = jnp.zeros_like(qT)
        q_scr[i, 0] = jnp.where(row < HEAD_DIM, qT, zero)
        q_scr[i, 1] = jnp.where(row >= HEAD_DIM, qT, zero)

    lqk = lqk_ref[...]
    e1 = jnp.exp(jnp.sum(lqk[0:1] * lqk[1:2], axis=-1, keepdims=True))
    e2 = jnp.exp(jnp.sum(lqk[2:3] * lqk[3:4], axis=-1, keepdims=True))
    lam = e1 - e2 + LAM_INIT_0

    def reset():
        m_scr[...] = jnp.full(m_scr.shape, NEG_INF, F32)
        acc_scr[...] = jnp.zeros(acc_scr.shape, F32)

    def scores(qi, j, bias_idx, s_scr, mb_scr):
        kc = k_ref[0, pl.ds(pl.multiple_of(j * t, t), t), :]
        for c in range(2):
            s = jnp.dot(kc, q_scr[qi, c], preferred_element_type=F32)
            if bias_idx is not None:
                s = s + bias_ref[0, bias_idx]
            s_scr[c] = s
            mb_scr[c] = jnp.max(s, axis=0, keepdims=True)

    ones_rows = jnp.where(lax.broadcasted_iota(jnp.int32, (BF16_ROWS, t), 0) == 0, 1.0, 0.0).astype(BF16)

    def consume(j, s_scr, mb_scr):
        vt = jnp.concatenate([vT_ref[0, j], ones_rows], axis=0)
        for c in range(2):
            m_old = m_scr[c]
            m_new = jnp.maximum(m_old, mb_scr[c])
            alpha = jnp.exp2(m_old - m_new)
            m_scr[c] = m_new
            p = jnp.exp2(s_scr[c] - m_new).astype(BF16)
            acc_scr[c] = alpha * acc_scr[c] + jnp.dot(vt, p, preferred_element_type=F32)

    def finalize(qi):
        l0 = acc_scr[0, V_DIM:V_DIM + 1, :]
        l1 = acc_scr[1, V_DIM:V_DIM + 1, :]
        o = acc_scr[0, :V_DIM, :] / l0 - lam * (acc_scr[1, :V_DIM, :] / l1)
        ms = jnp.mean(o * o, axis=0, keepdims=True)
        on = o * lax.rsqrt(ms + SUBLN_EPS) * g_ref[...] * (1.0 - LAM_INIT_0)
        o_ref[0, pl.ds(pl.multiple_of(qi * t, t), t), :] = on.T.astype(BF16)

    a_bufs = (sa_scr, mba_scr)
    b_bufs = (sb_scr, mbb_scr)
    c_bufs = (sc_scr, mbc_scr)

    reset()
    scores(0, 0, 0, *c_bufs)
    consume(0, *c_bufs)
    finalize(0)
    if n == 1:
        return
    scores(1, 1, 0, *c_bufs)

    def tile(qi, carry):
        reset()
        scores(qi, qi - 1, 1, *a_bufs)
        consume(qi, *c_bufs)
        n_far = qi - 1

        def far_steps(first, count):
            for s in range(count):
                into, held = (b_bufs, a_bufs) if s % 2 == 0 else (a_bufs, b_bufs)
                scores(qi, first - 1 - s, None, *into)
                consume(first - s, *held)

        def trip(pp, c2):
            far_steps(n_far - ATTN_UNROLL * pp, ATTN_UNROLL)
            return c2

        lax.fori_loop(0, n_far // ATTN_UNROLL, trip, 0)
        nxt = jnp.minimum(qi + 1, n - 1)
        rem = n_far % ATTN_UNROLL

        for r in range(ATTN_UNROLL):
            @pl.when(rem == r)
            def _(r=r):
                far_steps(r, r)
                scores(nxt, nxt, 0, *c_bufs)
                consume(0, *(a_bufs if r % 2 == 0 else b_bufs))

        finalize(qi)
        return carry

    lax.fori_loop(1, n, tile, 0)


def _attention(lqk, qT, k, vT, bias, g_col, t):
    B, n, D, _ = qT.shape
    S = n * t
    H = D // V_DIM
    kern = functools.partial(_attn_kernel, t=t, n=n)
    return pl.pallas_call(
        kern,
        grid=(B, H),
        in_specs=[
            _const_spec(lqk.shape),
            pl.BlockSpec((1, n, V_DIM, t), lambda b, h: (b, 0, h, 0)),
            pl.BlockSpec((1, S, V_DIM), lambda b, h: (b, 0, h)),
            pl.BlockSpec((1, n, V_DIM, t), lambda b, h: (b, 0, h, 0)),
            pl.BlockSpec((1, 2, t, t), lambda b, h: (h, 0, 0, 0)),
            _const_spec((V_DIM, 1)),
        ],
        out_specs=pl.BlockSpec((1, S, V_DIM), lambda b, h: (b, 0, h)),
        out_shape=jax.ShapeDtypeStruct((B, S, D), BF16),
        scratch_shapes=[
            pltpu.VMEM((n, 2, V_DIM, t), BF16),
            pltpu.VMEM((2, t, t), F32),
            pltpu.VMEM((2, t, t), F32),
            pltpu.VMEM((2, t, t), F32),
            pltpu.VMEM((2, 1, t), F32),
            pltpu.VMEM((2, 1, t), F32),
            pltpu.VMEM((2, 1, t), F32),
            pltpu.VMEM((2, 1, t), F32),
            pltpu.VMEM((2, V_DIM + BF16_ROWS, t), F32),
        ],
        compiler_params=pltpu.CompilerParams(
            dimension_semantics=("arbitrary", "arbitrary"),
            vmem_limit_bytes=_vmem_limit(52 * 1024 * 1024)),
        name="attn",
    )(lqk, qT, k, vT, bias, g_col)


def _swiglu_acc(h, wg, wu, wd, acc):
    width = wg.shape[-1]
    step = FF_SLAB if width % FF_SLAB == 0 else width
    for c in range(width // step):
        sl = slice(c * step, (c + 1) * step)
        gt = jnp.dot(h, wg[:, sl], preferred_element_type=F32)
        up = jnp.dot(h, wu[:, sl], preferred_element_type=F32)
        a = (gt * jax.nn.sigmoid(gt) * up).astype(BF16)
        acc = acc + jnp.dot(a, wd[sl, :], preferred_element_type=F32)
    return acc


def _ffn_kernel(o_ref, x_ref, wo_ref, g1_ref, wg_ref, wu_ref, wd_ref, g2_ref, x2_ref, h2_ref):
    x1 = x_ref[...] + jnp.dot(o_ref[...], wo_ref[...], preferred_element_type=F32)
    h1 = _rms(x1, g1_ref[...], NORM_EPS).astype(BF16)
    x2 = _swiglu_acc(h1, wg_ref, wu_ref, wd_ref, x1)
    x2_ref[...] = x2
    h2_ref[...] = _rms(x2, g2_ref[...], NORM_EPS).astype(BF16)


def _ffn(o, x, wo, g1, wg, wu, wd, g2, tm):
    T, D = x.shape
    Fd = wg.shape[1]
    kern = _ffn_kernel
    tile = lambda i: (i, 0)
    return pl.pallas_call(
        kern,
        grid=(T // tm,),
        in_specs=[
            pl.BlockSpec((tm, D), tile),
            pl.BlockSpec((tm, D), tile),
            _const_spec((D, D)),
            _const_spec((1, D)),
            _const_spec((D, Fd)),
            _const_spec((D, Fd)),
            _const_spec((Fd, D)),
            _const_spec((1, D)),
        ],
        out_specs=[pl.BlockSpec((tm, D), tile), pl.BlockSpec((tm, D), tile)],
        out_shape=[jax.ShapeDtypeStruct((T, D), F32), jax.ShapeDtypeStruct((T, D), BF16)],
        compiler_params=pltpu.CompilerParams(
            dimension_semantics=("arbitrary",),
            vmem_limit_bytes=_vmem_limit(56 * 1024 * 1024)),
        name="ffn",
    )(o, x, wo, g1, wg, wu, wd, g2)


def _gelu_tanh(x):
    return 0.5 * x * (1.0 + jnp.tanh(math.sqrt(2.0 / math.pi) * (x + 0.044715 * (x * x * x))))


def _lru_kernel(h2_ref, h2n_ref, x2_ref, win_ref, cw_ref, cb_ref, gaw_ref, gab_ref, gxw_ref,
                gxb_ref, lam_ref, wout_ref, gn_ref, rT_ref, x3_ref, h3_ref, lg_ref,
                gx0_scr, gx1_scr, xbuf, a_scr, b_scr, h_scr, hcar, *, ts, dr):
    si = pl.program_id(1)
    halo = SUBLANES

    @pl.when(si == 0)
    def _():
        xbuf[0:halo, :] = jnp.zeros((halo, dr), F32)
        hcar[...] = jnp.zeros(hcar.shape, F32)

    @pl.when((pl.program_id(0) == 0) & (si == 0))
    def _():
        gx0_scr[...] = jnp.dot(h2_ref[0, 0:ts, :], win_ref[...], preferred_element_type=F32)

    z = -lam_ref[...]
    ez = jnp.exp(-jnp.abs(z))
    u = 1.0 + ez
    du = u - 1.0
    log1p_ez = jnp.where(du == 0.0, ez, jnp.log(u) * (ez / du))
    softplus = jnp.maximum(z, 0.0) + log1p_ez

    def in_proj_slabs(load_rows, gx_scr):
        def slab(c0):
            def emit():
                gx_scr[:, c0:c0 + PROJ_SLAB] = jnp.dot(
                    load_rows(), win_ref[:, c0:c0 + PROJ_SLAB], preferred_element_type=F32)
            return emit
        return [slab(c0) for c0 in range(0, 2 * dr, PROJ_SLAB)]

    _lru_tile(0, gx0_scr, in_proj_slabs(lambda: h2_ref[0, ts:2 * ts, :], gx1_scr), softplus,
              x2_ref, cw_ref, cb_ref, gaw_ref, gab_ref, gxw_ref, gxb_ref,
              wout_ref, gn_ref, rT_ref, x3_ref, h3_ref, lg_ref, xbuf, a_scr, b_scr, h_scr, hcar,
              ts=ts, dr=dr)
    _lru_tile(ts, gx1_scr, in_proj_slabs(lambda: h2n_ref[0], gx0_scr), softplus,
              x2_ref, cw_ref, cb_ref, gaw_ref, gab_ref, gxw_ref, gxb_ref,
              wout_ref, gn_ref, rT_ref, x3_ref, h3_ref, lg_ref, xbuf, a_scr, b_scr, h_scr, hcar,
              ts=ts, dr=dr)


def _lru_tile(r0, gx_scr, background, softplus, x2_ref, cw_ref, cb_ref, gaw_ref, gab_ref, gxw_ref,
              gxb_ref, wout_ref, gn_ref, rT_ref, x3_ref, h3_ref, lg_ref, xbuf, a_scr, b_scr, h_scr,
              hcar, *, ts, dr):
    halo = SUBLANES
    pending = list(background)
    per_phase = -(-len(pending) // 10)

    def emit_background():
        for _ in range(min(per_phase, len(pending))):
            pending.pop(0)()

    emit_background()
    gate = _gelu_tanh(gx_scr[:, :dr])
    xb = gx_scr[:, dr:]
    xbuf[halo:halo + ts, :] = xb

    emit_background()
    cw = cw_ref[...]
    xc = cb_ref[...] + cw[CONV_WIDTH - 1:CONV_WIDTH] * xb
    for j in range(CONV_WIDTH - 1):
        back = CONV_WIDTH - 1 - j
        xc = xc + cw[j:j + 1] * xbuf[halo - back:halo - back + ts, :]
    xbuf[0:halo, :] = xb[ts - halo:ts, :]

    emit_background()
    xcb = xc.astype(BF16)
    r_parts, i_parts = [], []
    for n in range(dr // LRU_BLOCK):
        blk = xcb[:, n * LRU_BLOCK:(n + 1) * LRU_BLOCK]
        r_parts.append(jnp.dot(blk, gaw_ref[n], preferred_element_type=F32))
        i_parts.append(jnp.dot(blk, gxw_ref[n], preferred_element_type=F32))
    emit_background()
    r = jax.nn.sigmoid(jnp.concatenate(r_parts, axis=1) + gab_ref[...])
    emit_background()
    ig = jax.nn.sigmoid(jnp.concatenate(i_parts, axis=1) + gxb_ref[...])
    emit_background()
    log_a = (-LRU_C) * r * softplus
    a = jnp.exp(log_a)
    b = jnp.sqrt(1.0 - a * a) * (ig * xc)

    emit_background()
    ng = ts // SUBLANES
    a3 = a.reshape(ng, SUBLANES, dr)
    b3 = b.reshape(ng, SUBLANES, dr)
    sub = lax.broadcasted_iota(jnp.int32, a3.shape, 1)
    d = 1
    while d < SUBLANES:
        keep = sub >= d
        a_s = pltpu.roll(a3, d, axis=1)
        b_s = pltpu.roll(b3, d, axis=1)
        b3 = jnp.where(keep, a3 * b_s + b3, b3)
        a3 = jnp.where(keep, a3 * a_s, a3)
        d *= 2
        emit_background()
    a_scr[...] = a3.reshape(ts, dr)
    b_scr[...] = b3.reshape(ts, dr)

    emit_background()
    assert not pending
    hprev = hcar[...]
    for g in range(ng):
        rows = slice(g * SUBLANES, (g + 1) * SUBLANES)
        hg = a_scr[rows, :] * hprev + b_scr[rows, :]
        h_scr[rows, :] = hg
        hprev = jnp.broadcast_to(hg[SUBLANES - 1:SUBLANES, :], (SUBLANES, dr))
    hcar[...] = hprev

    y = (h_scr[...] * gate).astype(BF16)
    x3 = x2_ref[0, r0:r0 + ts, :] + jnp.dot(y, wout_ref[...], preferred_element_type=F32)
    x3_ref[0, r0:r0 + ts, :] = x3

    h3 = _rms(x3, gn_ref[...], NORM_EPS)
    h3_ref[0, r0:r0 + ts, :] = h3
    h_hi = h3.astype(BF16)
    h_lo = (h3 - h_hi.astype(F32)).astype(BF16)
    rT = rT_ref[...]
    r_hi = rT.astype(BF16)
    r_lo = (rT - r_hi.astype(F32)).astype(BF16)
    lg = lax.dot_general(r_hi, h_hi, NT_DIMS, preferred_element_type=F32)
    lg = lg + lax.dot_general(r_lo, h_hi, NT_DIMS, preferred_element_type=F32)
    lg = lg + lax.dot_general(r_hi, h_lo, NT_DIMS, preferred_element_type=F32)
    lg_ref[0, :, r0:r0 + ts] = lg


def _lru(h2, x2, win, cw, cb, gaw, gab, gxw, gxb, lam, wout, gn, rT, ts):
    B, S, D = x2.shape
    dr = wout.shape[0]
    E = rT.shape[0]
    n = S // (2 * ts)
    kern = functools.partial(_lru_kernel, ts=ts, dr=dr)
    tile = lambda b, i: (b, i, 0)

    def next_first(b, i):
        last = i + 1 >= n
        return (jnp.where(last, jnp.minimum(b + 1, B - 1), b), jnp.where(last, 0, 2 * (i + 1)), 0)

    return pl.pallas_call(
        kern,
        grid=(B, n),
        in_specs=[
            pl.BlockSpec((1, 2 * ts, D), tile),
            pl.BlockSpec((1, ts, D), next_first),
            pl.BlockSpec((1, 2 * ts, D), tile),
            _const_spec(win.shape),
            _const_spec(cw.shape),
            _const_spec(cb.shape),
            _const_spec(gaw.shape),
            _const_spec(gab.shape),
            _const_spec(gxw.shape),
            _const_spec(gxb.shape),
            _const_spec(lam.shape),
            _const_spec(wout.shape),
            _const_spec(gn.shape),
            _const_spec(rT.shape),
        ],
        out_specs=[pl.BlockSpec((1, 2 * ts, D), tile),
                   pl.BlockSpec((1, 2 * ts, D), tile),
                   pl.BlockSpec((1, E, 2 * ts), lambda b, i: (b, 0, i))],
        out_shape=[jax.ShapeDtypeStruct((B, S, D), F32),
                   jax.ShapeDtypeStruct((B, S, D), F32),
                   jax.ShapeDtypeStruct((B, E, S), F32)],
        scratch_shapes=[
            pltpu.VMEM((ts, 2 * dr), F32),
            pltpu.VMEM((ts, 2 * dr), F32),
            pltpu.VMEM((SUBLANES + ts, dr), F32),
            pltpu.VMEM((ts, dr), F32),
            pltpu.VMEM((ts, dr), F32),
            pltpu.VMEM((ts, dr), F32),
            pltpu.VMEM((SUBLANES, dr), F32),
        ],
        compiler_params=pltpu.CompilerParams(
            dimension_semantics=("arbitrary", "arbitrary"),
            vmem_limit_bytes=_vmem_limit(48 * 1024 * 1024)),
        name="lru",
    )(h2, h2, x2, win, cw, cb, gaw, gab, gxw, gxb, lam, wout, gn, rT)


def _route_kernel(lg_ref, idx_ref, gate_ref, cnt_ref, carry, *, tr):
    @pl.when(pl.program_id(0) == 0)
    def _():
        carry[...] = jnp.zeros(carry.shape, F32)

    lg = lg_ref[...]
    E = lg.shape[0]
    eidx = lax.broadcasted_iota(jnp.int32, lg.shape, 0).astype(F32)
    m1 = jnp.max(lg, axis=0, keepdims=True)
    i1 = jnp.min(jnp.where(lg == m1, eidx, float(E)), axis=0, keepdims=True)
    oh1 = eidx == i1
    lg2 = jnp.where(oh1, -jnp.inf, lg)
    m2 = jnp.max(lg2, axis=0, keepdims=True)
    i2 = jnp.min(jnp.where(lg2 == m2, eidx, float(E)), axis=0, keepdims=True)
    oh2 = eidx == i2
    ex = jnp.exp(m2 - m1)
    g1 = 1.0 / (1.0 + ex)
    g2 = ex / (1.0 + ex)

    both = jnp.where(oh1 | oh2, 1.0, 0.0).astype(BF16)
    ti = lax.broadcasted_iota(jnp.int32, (tr, tr), 0)
    tj = lax.broadcasted_iota(jnp.int32, (tr, tr), 1)
    before = jnp.where(ti < tj, 1.0, 0.0).astype(BF16)
    rank = jnp.dot(both, before, preferred_element_type=F32) + carry[...][:, 0:1]
    r1 = jnp.sum(jnp.where(oh1, rank, 0.0), axis=0, keepdims=True)
    r2 = jnp.sum(jnp.where(oh2, rank, 0.0), axis=0, keepdims=True)
    tot = carry[...] + jnp.sum(both.astype(F32), axis=1, keepdims=True)
    carry[...] = tot
    cnt_ref[...] = tot.astype(jnp.int32)

    orow = lax.broadcasted_iota(jnp.int32, (8, tr), 0)
    packed = jnp.where(orow == 0, i1, jnp.where(orow == 1, i2, jnp.where(orow == 2, r1, r2)))
    idx_ref[...] = packed.astype(jnp.int32)
    gate_ref[...] = jnp.where(orow == 0, g1, g2)


def _route(lgT, tr):
    E, T = lgT.shape
    kern = functools.partial(_route_kernel, tr=tr)
    return pl.pallas_call(
        kern,
        grid=(T // tr,),
        in_specs=[pl.BlockSpec((E, tr), lambda i: (0, i))],
        out_specs=[pl.BlockSpec((8, tr), lambda i: (0, i)),
                   pl.BlockSpec((8, tr), lambda i: (0, i)),
                   pl.BlockSpec((E, 128), lambda i: (0, 0))],
        out_shape=[jax.ShapeDtypeStruct((8, T), jnp.int32),
                   jax.ShapeDtypeStruct((8, T), F32),
                   jax.ShapeDtypeStruct((E, 128), jnp.int32)],
        scratch_shapes=[pltpu.VMEM((E, 128), F32)],
        compiler_params=pltpu.CompilerParams(dimension_semantics=("arbitrary",)),
        name="route",
    )(lgT)


def _dispatch_kernel(zt_ref, pos_ref, x_ref, xs_out, zero_scr, sem, zsem, *, td, rm):

    @pl.when(pl.program_id(0) == 0)
    def _():
        zero_scr[...] = jnp.zeros(zero_scr.shape, F32)
        for i in range(zt_ref.shape[0]):
            dst = xs_out.at[pl.ds(pl.multiple_of(zt_ref[i] * rm, rm), rm)]
            cp = pltpu.make_async_copy(zero_scr, dst, zsem)
            cp.start()
            cp.wait()

    def row_copy(g, u, k, dst_row):
        return pltpu.make_async_copy(x_ref.at[g, pl.ds(u, 1)], xs_out.at[pl.ds(dst_row, 1)], sem)

    def issue(g, c):
        for u in range(SUBLANES):
            for k in range(2):
                row_copy(g, u, k, pos_ref[g, 2 * u + k]).start(priority=k)
        return c

    lax.fori_loop(0, td // SUBLANES, issue, 0)

    def drain(g, c):
        for u in range(SUBLANES):
            for k in range(2):
                row_copy(0, 0, k, 0).wait()
        return c

    lax.fori_loop(0, td // SUBLANES, drain, 0)


def _dispatch(zero_tiles, pos_g, x, n_rows, td, rm):
    T, D = x.shape
    kern = functools.partial(_dispatch_kernel, td=td, rm=rm)
    gd = td // SUBLANES
    return pl.pallas_call(
        kern,
        grid_spec=pltpu.PrefetchScalarGridSpec(
            num_scalar_prefetch=1,
            grid=(T // td,),
            in_specs=[
                pl.BlockSpec((gd, 2 * SUBLANES), lambda i, zt: (i, 0), memory_space=pltpu.SMEM),
                pl.BlockSpec((gd, SUBLANES, D), lambda i, zt: (i, 0, 0)),
            ],
            out_specs=pl.BlockSpec(memory_space=pl.ANY),
            scratch_shapes=[pltpu.VMEM((rm, D), F32), pltpu.SemaphoreType.DMA,
                            pltpu.SemaphoreType.DMA],
        ),
        out_shape=jax.ShapeDtypeStruct((n_rows, D), F32),
        compiler_params=pltpu.CompilerParams(
            dimension_semantics=("arbitrary",), has_side_effects=True),
        name="dispatch",
    )(zero_tiles, pos_g, x.reshape(T // SUBLANES, SUBLANES, D))


def _expert_kernel(te_ref, nu_ref, xs_ref, wg_ref, wu_ref, wd_ref, ys_ref, *, nf):
    i = pl.program_id(0)
    j = pl.program_id(1)

    @pl.when(j == 0)
    def _():
        ys_ref[...] = jnp.zeros(ys_ref.shape, F32)

    @pl.when(i < nu_ref[0])
    def _():
        ys_ref[...] = _swiglu_acc(xs_ref[...].astype(BF16), wg_ref.at[0], wu_ref.at[0],
                                  wd_ref.at[0], ys_ref[...])


def _experts(tile_expert, n_used, xs, wg, wu, wd, tm, nf):
    NP, D = xs.shape
    E, _, Fd = wg.shape
    fc = Fd // nf
    nt = NP // tm

    def eff(i, nu):
        return jnp.minimum(i, nu[0] - 1)

    def chunk(i, j, nu):
        ie = eff(i, nu)
        je = jnp.where(i < nu[0], j, nf - 1)
        return jnp.where(ie % 2 == 0, je, nf - 1 - je)

    kern = functools.partial(_expert_kernel, nf=nf)
    return pl.pallas_call(
        kern,
        grid_spec=pltpu.PrefetchScalarGridSpec(
            num_scalar_prefetch=2,
            grid=(nt, nf),
            in_specs=[
                pl.BlockSpec((tm, D), lambda i, j, te, nu: (eff(i, nu), 0)),
                pl.BlockSpec((1, D, fc), lambda i, j, te, nu: (te[eff(i, nu)], 0, chunk(i, j, nu))),
                pl.BlockSpec((1, D, fc), lambda i, j, te, nu: (te[eff(i, nu)], 0, chunk(i, j, nu))),
                pl.BlockSpec((1, fc, D), lambda i, j, te, nu: (te[eff(i, nu)], chunk(i, j, nu), 0)),
            ],
            out_specs=pl.BlockSpec((tm, D), lambda i, j, te, nu: (i, 0)),
        ),
        out_shape=jax.ShapeDtypeStruct((NP, D), F32),
        compiler_params=pltpu.CompilerParams(
            dimension_semantics=("arbitrary", "arbitrary"),
            vmem_limit_bytes=_vmem_limit(56 * 1024 * 1024)),
        name="experts",
    )(tile_expert, n_used, xs, wg, wu, wd)


def _combine_kernel(pos_ref, gate_ref, x_ref, gn_ref, ys_hbm, out_ref, buf, sem, *, tc):
    def row_copy(g, u, k, src_row):
        return pltpu.make_async_copy(ys_hbm.at[pl.ds(src_row, 1)], buf.at[k, g, pl.ds(u, 1)], sem)

    def issue(g, c):
        for u in range(SUBLANES):
            for k in range(2):
                row_copy(g, u, k, pos_ref[g, 2 * u + k]).start(priority=k)
        return c

    lax.fori_loop(0, tc // SUBLANES, issue, 0)

    def drain(g, c):
        for u in range(SUBLANES):
            for k in range(2):
                row_copy(0, 0, k, 0).wait()
        return c

    lax.fori_loop(0, tc // SUBLANES, drain, 0)

    g = gate_ref[...]
    D = x_ref.shape[-1]
    x4 = (x_ref[...] + g[:, 0:1] * buf[0].reshape(tc, D) + g[:, 1:2] * buf[1].reshape(tc, D))
    out_ref[...] = _rms(x4, gn_ref[...], NORM_EPS)


def _combine(pos_g, gates_col, x, gn, ys, tc):
    T, D = x.shape
    kern = functools.partial(_combine_kernel, tc=tc)
    gc = tc // SUBLANES
    return pl.pallas_call(
        kern,
        grid=(T // tc,),
        in_specs=[
            pl.BlockSpec((gc, 2 * SUBLANES), lambda i: (i, 0), memory_space=pltpu.SMEM),
            pl.BlockSpec((tc, 2), lambda i: (i, 0)),
            pl.BlockSpec((tc, D), lambda i: (i, 0)),
            _const_spec((1, D)),
            pl.BlockSpec(memory_space=pl.ANY),
        ],
        out_specs=pl.BlockSpec((tc, D), lambda i: (i, 0)),
        out_shape=jax.ShapeDtypeStruct((T, D), F32),
        scratch_shapes=[pltpu.VMEM((2, gc, SUBLANES, D), F32), pltpu.SemaphoreType.DMA],
        compiler_params=pltpu.CompilerParams(
            dimension_semantics=("arbitrary",),
            vmem_limit_bytes=_vmem_limit(40 * 1024 * 1024)),
        name="combine",
    )(pos_g, gates_col, x, gn, ys)


def _tile(n, pref):
    t = min(n, pref)
    assert n % t == 0, (n, t)
    return t


def kernel(x, norm_mix, norm_ffn, norm_final, rel_bias, attn_w_qkv, attn_w_o, attn_lambda_q1, attn_lambda_k1, attn_lambda_q2, attn_lambda_k2, attn_subln, lru_w_in, lru_conv_w, lru_conv_b, lru_gate_a_w, lru_gate_a_b, lru_gate_x_w, lru_gate_x_b, lru_lambda, lru_w_out, ffn_w_gate, ffn_w_up, ffn_w_down, moe_router, moe_w_gate, moe_w_up, moe_w_down):
    B, S, D = x.shape
    T = B * S
    assert D == N_HEADS * V_DIM
    row = lambda v: v.reshape(1, -1).astype(F32)

    ta = _tile(S, ATTN_TILE)
    wqkv = attn_w_qkv[0]
    wqT = (wqkv[:, :D] * (SCALE * LOG2E)).T.astype(BF16)
    wk = wqkv[:, D:2 * D].astype(BF16)
    wvT = wqkv[:, 2 * D:].T.astype(BF16)
    qT, k, vT = _qkv(x, row(norm_mix[0]), wqT, wk, wvT, ta)
    lqk = jnp.stack([attn_lambda_q1[0], attn_lambda_k1[0],
                     attn_lambda_q2[0], attn_lambda_k2[0]]).astype(F32)
    bias = _bias_tiles(rel_bias, ta)
    o = _attention(lqk, qT, k, vT, bias, attn_subln[0].reshape(V_DIM, 1).astype(F32), ta)

    tm = _tile(T, TOKEN_TILE)
    x2, h2 = _ffn(o.reshape(T, D), x.reshape(T, D), attn_w_o[0].astype(BF16), row(norm_ffn[0]),
                  ffn_w_gate[0].astype(BF16), ffn_w_up[0].astype(BF16),
                  ffn_w_down[0].astype(BF16), row(norm_mix[1]), tm)

    ts = _tile(S, LRU_TILE)
    x3, h3, lg = _lru(h2.reshape(B, S, D), x2.reshape(B, S, D), lru_w_in[0].astype(BF16),
                      lru_conv_w[0].astype(F32), row(lru_conv_b[0]),
                      lru_gate_a_w[0].astype(BF16), row(lru_gate_a_b[0]),
                      lru_gate_x_w[0].astype(BF16), row(lru_gate_x_b[0]),
                      row(lru_lambda[0]), lru_w_out[0].astype(BF16), row(norm_ffn[1]),
                      moe_router[0].T.astype(F32), ts)
    x3 = x3.reshape(T, D)
    h3 = h3.reshape(T, D)
    E = moe_router.shape[-1]
    lgT = jnp.transpose(lg, (1, 0, 2)).reshape(E, T)

    idx, gates, counts = _route(lgT, _tile(T, ROUTE_TILE))
    rm = EXPERT_ROW_TILE
    cnt = counts[:, 0]
    padded = ((cnt + rm - 1) // rm) * rm
    ends = jnp.cumsum(padded)
    starts = ends - padded
    pos = jnp.stack([starts[idx[0]] + idx[2], starts[idx[1]] + idx[3]])
    n_tiles = (2 * T) // rm + E
    tile_rows = jnp.arange(n_tiles, dtype=jnp.int32) * rm
    tile_expert = jnp.minimum(
        jnp.sum((tile_rows[:, None] >= ends[None, :]).astype(jnp.int32), axis=1), E - 1)
    n_used = (ends[-1] // rm).astype(jnp.int32).reshape(1)

    last_tile = jnp.maximum(ends // rm - 1, 0)
    tail_tile = jnp.minimum(n_used[0] + jnp.arange(E, dtype=jnp.int32), n_tiles - 1)
    zero_tiles = jnp.concatenate([last_tile, tail_tile]).astype(jnp.int32)

    td = _tile(T, DISPATCH_TILE)
    pos_g = pos.T.reshape(T // SUBLANES, 2 * SUBLANES)
    xs = _dispatch(zero_tiles, pos_g, h3, n_tiles * rm, td, rm)
    ys = _experts(tile_expert, n_used, xs, moe_w_gate[0].astype(BF16),
                  moe_w_up[0].astype(BF16), moe_w_down[0].astype(BF16), rm, EXPERT_FF_CHUNKS)
    out = _combine(pos_g, gates[:2].T, x3, row(norm_final), ys, td)
    return out.reshape(B, S, D)
```
